```python
import math
import functools
import jax
import jax.numpy as jnp
from jax import lax
import numpy as np

D_MODEL = 1024
BATCH = 16
SEQ = 4096
DEPTH = 1
DEC_BATCH = 128
DEC_SEQ = 1
PAST_LEN = 8192
PAGE_SIZE = 128

N_HEADS = 8
N_KV_HEADS = 4
GQA = N_HEADS // N_KV_HEADS
HEAD_DIM = D_MODEL // N_HEADS // 2
V_DIM = 2 * HEAD_DIM
ROPE_DIM = HEAD_DIM // 4
ROPE_THETA = 500000.0
Q_BLOCK = 128
GMLP_WIDTH = D_MODEL
GMLP_GROUPS = 8
GMLP_GROUP_DIM = GMLP_WIDTH // GMLP_GROUPS
CHUNK = 128
N_GROUPS = 4
EXPERTS_PER_GROUP = 8
N_EXPERTS = N_GROUPS * EXPERTS_PER_GROUP
TOP_K = 2
D_EXPERT = D_MODEL // 2
MOE_BLOCK = 128
Q_COLS = N_HEADS * 2 * HEAD_DIM
K_COLS = N_KV_HEADS * 2 * HEAD_DIM
V_COLS = N_KV_HEADS * V_DIM
IN_COLS = Q_COLS + K_COLS + V_COLS + 2 * GMLP_WIDTH + 2 * D_MODEL
SPLITS = [Q_COLS, Q_COLS + K_COLS, Q_COLS + K_COLS + V_COLS,
          Q_COLS + K_COLS + V_COLS + GMLP_WIDTH, Q_COLS + K_COLS + V_COLS + 2 * GMLP_WIDTH,
          Q_COLS + K_COLS + V_COLS + 2 * GMLP_WIDTH + D_MODEL]
EPS = 1e-6
NEG_INF = -1e30

kernel_name = 'hybrid_diffattn_gmlp_hmoe_step'


def _rmsnorm(x, g):
    xf = x.astype(jnp.float32)
    y = xf * lax.rsqrt(jnp.mean(xf * xf, axis=-1, keepdims=True) + EPS)
    return (y * g.astype(jnp.float32)).astype(x.dtype)


def _layernorm(x, g, b):
    xf = x.astype(jnp.float32)
    xc = xf - jnp.mean(xf, axis=-1, keepdims=True)
    y = xc * lax.rsqrt(jnp.mean(xc * xc, axis=-1, keepdims=True) + EPS)
    return (y * g.astype(jnp.float32) + b.astype(jnp.float32)).astype(x.dtype)


def _adaln(c, w, b):
    mod = jnp.dot(jax.nn.silu(c), w) + b
    return mod.reshape(c.shape[0], -1, D_MODEL)


def _modulate(h, shift, scale):
    return h * (1 + scale[:, None, :]) + shift[:, None, :]


def _rope(x, pos):
    half = ROPE_DIM // 2
    inv_freq = ROPE_THETA ** (-jnp.arange(0, ROPE_DIM, 2, dtype=jnp.float32) / ROPE_DIM)
    ang = pos.astype(jnp.float32)[:, None] * inv_freq[None, :]
    bshape = (1, pos.shape[0]) + (1,) * (x.ndim - 3) + (half,)
    cos = jnp.cos(ang).reshape(bshape)
    sin = jnp.sin(ang).reshape(bshape)
    xr = x[..., :ROPE_DIM].astype(jnp.float32)
    x1, x2 = xr[..., :half], xr[..., half:]
    rot = jnp.concatenate([x1 * cos - x2 * sin, x2 * cos + x1 * sin], axis=-1)
    return jnp.concatenate([rot.astype(x.dtype), x[..., ROPE_DIM:]], axis=-1)


def _diff_attend(q, parts, lam):
    scale = HEAD_DIM ** -0.5
    scores = []
    for k, _, mask in parts:
        s = jnp.einsum('bqngmd,bknmd->bngmqk', q, k, preferred_element_type=jnp.float32) * scale
        if mask is not None:
            s = jnp.where(mask, s, NEG_INF)
        scores.append(s)
    m = scores[0].max(axis=-1, keepdims=True)
    for s in scores[1:]:
        m = jnp.maximum(m, s.max(axis=-1, keepdims=True))
    exps = [jnp.exp(s - m) for s in scores]
    den = exps[0].sum(axis=-1, keepdims=True)
    for e in exps[1:]:
        den = den + e.sum(axis=-1, keepdims=True)
    out = None
    for e, (_, v, _) in zip(exps, parts):
        p = e / den
        a = p[:, :, :, 0] - lam * p[:, :, :, 1]
        o = jnp.einsum('bngqk,bknd->bqngd', a.astype(v.dtype), v)
        out = o if out is None else out + o
    return out


def _attend_prompt(q, k, v, lam):
    b, s = q.shape[0], q.shape[1]
    nb = s // Q_BLOCK
    qb = jnp.moveaxis(q.reshape((b, nb, Q_BLOCK) + q.shape[2:]), 1, 0)
    kpos = jnp.arange(s)

    def one_block(args):
        qi, i = args
        qpos = i * Q_BLOCK + jnp.arange(Q_BLOCK)
        mask = qpos[:, None] >= kpos[None, :]
        return _diff_attend(qi, [(k, v, mask)], lam)

    o = lax.map(one_block, (qb, jnp.arange(nb)))
    return jnp.moveaxis(o, 0, 1).reshape((b, s) + o.shape[3:])


def _attend_sample(q, k, v, lam, cache_k, cache_v, page_table, layer):
    db, t = q.shape[0], q.shape[1]
    n_past = page_table.shape[1] * PAGE_SIZE
    k_past = cache_k[layer, page_table].reshape(db, n_past, N_KV_HEADS, 2, HEAD_DIM)
    v_past = cache_v[layer, page_table].reshape(db, n_past, N_KV_HEADS, V_DIM)
    causal = jnp.tril(jnp.ones((t, t), dtype=bool))
    return _diff_attend(q, [(k_past, v_past, None), (k, v, causal)], lam)


def _spatial_gating(gu, gv, g_ln, b_ln, w_s, b_s):
    b, s, _ = gu.shape
    L = min(s, CHUNK)
    u = jax.nn.gelu(gu)
    v = _layernorm(jax.nn.gelu(gv), g_ln, b_ln)
    vc = v.reshape(b, s // L, L, GMLP_GROUPS, GMLP_GROUP_DIM)
    w = w_s[:, :L, :L] * jnp.tril(jnp.ones((L, L), w_s.dtype))
    sp = jnp.einsum('gts,bcsgd->bctgd', w, vc) + b_s[:, :L].T[None, None, :, :, None]
    return u * sp.reshape(b, s, GMLP_WIDTH), v


def _moe(h, w_rg, b_rg, w_re, b_re, w_eg, w_eu, w_ed):
    b, s, d = h.shape
    T = b * s
    x2 = h.reshape(T, d)
    f32 = jnp.float32
    tix = jnp.arange(T)
    lg = jnp.dot(x2, w_rg).astype(f32) + b_rg.astype(f32)
    pg = jax.nn.softmax(lg, axis=-1)
    g_idx = jnp.argmax(lg, axis=-1).astype(jnp.int32)
    p_grp = pg[tix, g_idx]
    le = (jnp.dot(x2, w_re).astype(f32) + b_re.astype(f32)).reshape(T, N_GROUPS, EXPERTS_PER_GROUP)
    pe = jax.nn.softmax(le[tix, g_idx], axis=-1)
    top_p, top_i = lax.top_k(pe, TOP_K)
    top_p = top_p / top_p.sum(axis=-1, keepdims=True)
    experts = g_idx[:, None] * EXPERTS_PER_GROUP + top_i.astype(jnp.int32)
    weights = p_grp[:, None] * top_p
    TK = T * TOP_K
    e_flat = experts.reshape(TK)
    w_flat = weights.reshape(TK)
    tok_flat = jnp.repeat(jnp.arange(T, dtype=jnp.int32), TOP_K)
    order = jnp.argsort(e_flat)
    e_sorted = e_flat[order]
    counts = jnp.bincount(e_flat, length=N_EXPERTS)
    padded = (counts + MOE_BLOCK - 1) // MOE_BLOCK * MOE_BLOCK
    pad_end = jnp.cumsum(padded)
    pad_start = pad_end - padded
    start = jnp.cumsum(counts) - counts
    dest = pad_start[e_sorted] + jnp.arange(TK) - start[e_sorted]
    n_blocks = -(-TK // MOE_BLOCK) + N_EXPERTS
    slots = n_blocks * MOE_BLOCK
    slot_tok = jnp.full((slots,), T, jnp.int32).at[dest].set(tok_flat[order])
    slot_w = jnp.zeros((slots,), f32).at[dest].set(w_flat[order])
    block_expert = jnp.clip(jnp.searchsorted(pad_end, jnp.arange(n_blocks) * MOE_BLOCK, side='right'),
                            0, N_EXPERTS - 1)
    x_pad = jnp.concatenate([x2, jnp.zeros((1, d), x2.dtype)], axis=0)

    def run_block(args):
        toks, e = args
        xb = x_pad[toks]
        hid = jax.nn.silu(xb @ w_eg[e]) * (xb @ w_eu[e])
        return hid @ w_ed[e]

    yb = lax.map(run_block, (slot_tok.reshape(n_blocks, MOE_BLOCK), block_expert))
    y = jnp.zeros((T + 1, d), f32).at[slot_tok].add(yb.reshape(slots, d).astype(f32) * slot_w[:, None])
    return y[:T].astype(h.dtype).reshape(b, s, d)


def _layer(x, c, pos, attend, lam_init, w_ada, b_ada, g1, g2, w_in, lq1, lk1, lq2, lk2, g_subln,
           g_ln, b_ln, w_s, b_s, w_ba, w_bb, w_out, w_rg, b_rg, w_re, b_re, w_eg, w_eu, w_ed):
    b, s, _ = x.shape
    mod = _adaln(c, w_ada, b_ada)
    h = _modulate(_rmsnorm(x, g1), mod[:, 0], mod[:, 1])
    z = jnp.einsum('bsd,dc->bsc', h, w_in)
    q, k, v, gu, gv, ga, gb = jnp.split(z, SPLITS, axis=-1)
    q = _rope(q.reshape(b, s, N_HEADS, 2, HEAD_DIM), pos).reshape(b, s, N_KV_HEADS, GQA, 2, HEAD_DIM)
    k = _rope(k.reshape(b, s, N_KV_HEADS, 2, HEAD_DIM), pos)
    v = v.reshape(b, s, N_KV_HEADS, V_DIM)
    f32 = jnp.float32
    lam = (jnp.exp(jnp.sum(lq1.astype(f32) * lk1.astype(f32)))
           - jnp.exp(jnp.sum(lq2.astype(f32) * lk2.astype(f32))) + lam_init)
    o = attend(q, k, v, lam)
    o = _rmsnorm(o, g_subln) * (1.0 - lam_init)
    y_a = jnp.einsum('bsc,cd->bsd', o.reshape(b, s, N_HEADS * V_DIM), w_ba)
    sg, v_rows = _spatial_gating(gu, gv, g_ln, b_ln, w_s, b_s)
    y_b = jnp.einsum('bsc,cd->bsd', sg, w_bb)
    merged = jax.nn.sigmoid(ga) * y_a + jax.nn.sigmoid(gb) * y_b
    x = x + mod[:, 2][:, None, :] * jnp.einsum('bsc,cd->bsd', merged, w_out)
    h2 = _modulate(_rmsnorm(x, g2), mod[:, 3], mod[:, 4])
    x = x + mod[:, 5][:, None, :] * _moe(h2, w_rg, b_rg, w_re, b_re, w_eg, w_eu, w_ed)
    return x, k.reshape(b, s, N_KV_HEADS, 2 * HEAD_DIM), v, v_rows


def _final_norm(x, c, w, b, g):
    mod = _adaln(c, w, b)
    return _modulate(_rmsnorm(x, g), mod[:, 0], mod[:, 1])


def setup_inputs(seed: int = 0) -> dict:
    key = jax.random.key(seed)
    ks = iter(jax.random.split(key, 48))

    def nrm(shape, scale):
        return jax.random.normal(next(ks), shape, dtype=jnp.float32) * scale

    def gain(shape):
        return 1.0 + nrm(shape, 0.02)

    d = D_MODEL
    n_pages = PAST_LEN // PAGE_SIZE
    n_pool = (DEC_BATCH * n_pages * 5) // 4
    perm = jax.random.permutation(next(ks), n_pool)
    page_table = perm[: DEC_BATCH * n_pages].reshape(DEC_BATCH, n_pages).astype(jnp.int32)
    cache_k = jax.random.normal(next(ks), (DEPTH, n_pool, PAGE_SIZE, N_KV_HEADS, 2 * HEAD_DIM), dtype=jnp.float32)
    cache_v = jax.random.normal(next(ks), (DEPTH, n_pool, PAGE_SIZE, N_KV_HEADS, V_DIM), dtype=jnp.float32)
    return {
        'x_prompt': nrm((BATCH, SEQ, d), 1.0),
        'x_sample': nrm((DEC_BATCH, DEC_SEQ, d), 1.0),
        'cache_k': cache_k,
        'cache_v': cache_v,
        'page_table': page_table,
        'c_prompt': nrm((BATCH, d), 1.0),
        'c_sample': nrm((DEC_BATCH, d), 1.0),
        'w_ada': nrm((DEPTH, d, 6 * d), 0.5 * d ** -0.5),
        'b_ada': nrm((DEPTH, 6 * d), 0.02),
        'w_ada_final': nrm((d, 2 * d), 0.5 * d ** -0.5),
        'b_ada_final': nrm((2 * d,), 0.02),
        'g_norm1': gain((DEPTH, d)),
        'g_norm2': gain((DEPTH, d)),
        'g_final': gain((d,)),
        'w_in': nrm((DEPTH, d, IN_COLS), d ** -0.5),
        'lambda_q1': nrm((DEPTH, HEAD_DIM), 0.1),
        'lambda_k1': nrm((DEPTH, HEAD_DIM), 0.1),
        'lambda_q2': nrm((DEPTH, HEAD_DIM), 0.1),
        'lambda_k2': nrm((DEPTH, HEAD_DIM), 0.1),
        'g_subln': gain((DEPTH, V_DIM)),
        'g_gmlp_ln': gain((DEPTH, GMLP_WIDTH)),
        'b_gmlp_ln': nrm((DEPTH, GMLP_WIDTH), 0.02),
        'w_spatial': nrm((DEPTH, GMLP_GROUPS, CHUNK, CHUNK), CHUNK ** -0.5),
        'b_spatial': gain((DEPTH, GMLP_GROUPS, CHUNK)),
        'w_branch_a': nrm((DEPTH, N_HEADS * V_DIM, d), (N_HEADS * V_DIM) ** -0.5),
        'w_branch_b': nrm((DEPTH, GMLP_WIDTH, d), GMLP_WIDTH ** -0.5),
        'w_out': nrm((DEPTH, d, d), d ** -0.5),
        'w_router_group': nrm((DEPTH, d, N_GROUPS), d ** -0.5),
        'b_router_group': nrm((DEPTH, N_GROUPS), 0.01),
        'w_router_expert': nrm((DEPTH, d, N_EXPERTS), d ** -0.5),
        'b_router_expert': nrm((DEPTH, N_EXPERTS), 0.01),
        'w_exp_gate': nrm((DEPTH, N_EXPERTS, d, D_EXPERT), d ** -0.5),
        'w_exp_up': nrm((DEPTH, N_EXPERTS, d, D_EXPERT), d ** -0.5),
        'w_exp_down': nrm((DEPTH, N_EXPERTS, D_EXPERT, d), D_EXPERT ** -0.5),
    }


def reference(x_prompt, x_sample, cache_k, cache_v, page_table, c_prompt, c_sample, w_ada, b_ada,
              w_ada_final, b_ada_final, g_norm1, g_norm2, g_final, w_in, lambda_q1, lambda_k1,
              lambda_q2, lambda_k2, g_subln, g_gmlp_ln, b_gmlp_ln, w_spatial, b_spatial, w_branch_a,
              w_branch_b, w_out, w_router_group, b_router_group, w_router_expert, b_router_expert,
              w_exp_gate, w_exp_up, w_exp_down):
    n_past = page_table.shape[1] * PAGE_SIZE
    pos_prompt = jnp.arange(x_prompt.shape[1], dtype=jnp.int32)
    pos_sample = n_past + jnp.arange(x_sample.shape[1], dtype=jnp.int32)
    xp, xs = x_prompt, x_sample
    k_p, v_p, k_s, v_s, g_s = [], [], [], [], []
    for l in range(DEPTH):
        lam_init = 0.8 - 0.6 * math.exp(-0.3 * l)
        lw = (w_ada[l], b_ada[l], g_norm1[l], g_norm2[l], w_in[l], lambda_q1[l], lambda_k1[l],
              lambda_q2[l], lambda_k2[l], g_subln[l], g_gmlp_ln[l], b_gmlp_ln[l], w_spatial[l],
              b_spatial[l], w_branch_a[l], w_branch_b[l], w_out[l], w_router_group[l],
              b_router_group[l], w_router_expert[l], b_router_expert[l], w_exp_gate[l],
              w_exp_up[l], w_exp_down[l])
        xp, kp, vp, _ = _layer(xp, c_prompt, pos_prompt, _attend_prompt, lam_init, *lw)
        attend_s = functools.partial(_attend_sample, cache_k=cache_k, cache_v=cache_v,
                                     page_table=page_table, layer=l)
        xs, ks_new, vs_new, gv_new = _layer(xs, c_sample, pos_sample, attend_s, lam_init, *lw)
        k_p.append(kp)
        v_p.append(vp)
        k_s.append(ks_new)
        v_s.append(vs_new)
        g_s.append(gv_new)
    y_prompt = _final_norm(xp, c_prompt, w_ada_final, b_ada_final, g_final)
    y_sample = _final_norm(xs, c_sample, w_ada_final, b_ada_final, g_final)
    return (y_prompt, y_sample, jnp.stack(k_p), jnp.stack(v_p), jnp.stack(k_s), jnp.stack(v_s), jnp.stack(g_s))
```

```python
import functools
import math

import jax
import jax.numpy as jnp
from jax import lax
from jax.experimental import pallas as pl
from jax.experimental.pallas import tpu as pltpu

F32 = jnp.float32
BF16 = jnp.bfloat16
U32 = jnp.uint32
I32 = jnp.int32

D_MODEL = 1024
N_HEADS = 8
N_KV_HEADS = 4
GQA = N_HEADS // N_KV_HEADS
HEAD_DIM = 64
V_DIM = 128
ROPE_DIM = 16
ROPE_THETA = 500000.0
PAGE_SIZE = 128
GMLP_GROUPS = 8
GMLP_GROUP_DIM = 128
CHUNK = 128
N_GROUPS = 4
EXPERTS_PER_GROUP = 8
N_EXPERTS = 32
D_EXPERT = 512
Q_COLS = 1024
K_COLS = 512
V_COLS = 512
COL_Q, COL_K, COL_V, COL_GU, COL_GV, COL_GA, COL_GB = 0, 1024, 1536, 2048, 3072, 4096, 5120
EPS = 1e-6
NEG_INF = -1e30
LANES = 128
VMEM_LIMIT = 56 * 1024 * 1024

HIGHEST = lax.Precision.HIGHEST


def _params(sem):
    return pltpu.CompilerParams(dimension_semantics=sem, vmem_limit_bytes=VMEM_LIMIT)


def _rmsnorm(x, g):
    return x * lax.rsqrt(jnp.mean(x * x, axis=-1, keepdims=True) + EPS) * g


def _pack_pair(a, b):
    ua = lax.bitcast_convert_type(a.astype(BF16).astype(F32), U32)
    ub = lax.bitcast_convert_type(b.astype(BF16).astype(F32), U32)
    return (ua >> 16) | (ub & jnp.uint32(0xFFFF0000))


def _unpack_pair(w):
    a = lax.bitcast_convert_type(w << 16, F32)
    b = lax.bitcast_convert_type(w & jnp.uint32(0xFFFF0000), F32)
    return a, b


def _adaln_kernel(c_ref, w_ref, b_ref, o_ref):
    c = c_ref[...]
    a = c * jax.nn.sigmoid(c)
    o_ref[...] = jnp.dot(a, w_ref[...], preferred_element_type=F32, precision=HIGHEST) + b_ref[...]


def _adaln(c, w, b, tn=512):
    rows, d = c.shape
    n = w.shape[1]
    return pl.pallas_call(
        _adaln_kernel,
        grid=(n // tn,),
        in_specs=[pl.BlockSpec((rows, d), lambda j: (0, 0)),
                  pl.BlockSpec((d, tn), lambda j: (0, j)),
                  pl.BlockSpec((1, tn), lambda j: (0, j))],
        out_specs=pl.BlockSpec((rows, tn), lambda j: (0, j)),
        out_shape=jax.ShapeDtypeStruct((rows, n), F32),
        compiler_params=_params(("arbitrary",)),
        name="adaln",
    )(c, w, b.reshape(1, n))


def _lam_kernel(q1_ref, k1_ref, q2_ref, k2_ref, o_ref, *, lam_init):
    a = jnp.sum(q1_ref[...] * k1_ref[...], axis=-1, keepdims=True)
    b = jnp.sum(q2_ref[...] * k2_ref[...], axis=-1, keepdims=True)
    lam = jnp.exp(a) - jnp.exp(b) + lam_init
    o_ref[...] = jnp.broadcast_to(lam, o_ref.shape)


def _lam(q1, k1, q2, k2, lam_init):
    args = [a.reshape(1, HEAD_DIM) for a in (q1, k1, q2, k2)]
    return pl.pallas_call(
        functools.partial(_lam_kernel, lam_init=lam_init),
        out_shape=jax.ShapeDtypeStruct((8, LANES), F32),
        name="lam",
    )(*args)


def _inproj_kernel(x_ref, shift_ref, scale_ref, g1_ref, win_ref, rc_ref, ra_ref, rb_ref,
                   wsp_ref, bsp_ref, gln_ref, bln_ref, wbb_ref,
                   q_ref, kf_ref, vf_ref, kb_ref, vb_ref, sga_ref, gyb_ref, *rest,
                   chunked, emit_gv):
    if emit_gv:
        gv_ref, sg_scr = rest
    else:
        (sg_scr,) = rest
    tm = x_ref.shape[0]
    x = x_ref[...]
    h = _rmsnorm(x, g1_ref[...]) * (1.0 + scale_ref[...]) + shift_ref[...]
    hb = h.astype(BF16)

    def proj(c0, n):
        return jnp.dot(hb, win_ref[:, c0:c0 + n], preferred_element_type=F32)

    rc, ra, rb = rc_ref[...], ra_ref[...], rb_ref[...]

    def rope(z):
        return z * rc + pltpu.roll(z, LANES - ROPE_DIM // 2, 1) * ra + pltpu.roll(z, ROPE_DIM // 2, 1) * rb

    zq = proj(COL_Q, Q_COLS)
    for j in range(Q_COLS // LANES):
        z = rope(zq[:, j * LANES:(j + 1) * LANES])
        q_ref[:, j * LANES:(j + 1) * LANES] = (z * (HEAD_DIM ** -0.5)).astype(BF16)
    zk = proj(COL_K, K_COLS)
    for j in range(K_COLS // LANES):
        z = rope(zk[:, j * LANES:(j + 1) * LANES])
        kf_ref[:, j * LANES:(j + 1) * LANES] = z
        kb_ref[:, j * LANES:(j + 1) * LANES] = z.astype(BF16)
    zv = proj(COL_V, V_COLS)
    vf_ref[...] = zv
    vb_ref[...] = zv.astype(BF16)

    u = jax.nn.gelu(proj(COL_GU, D_MODEL))
    gv = jax.nn.gelu(proj(COL_GV, D_MODEL))
    xc = gv - jnp.mean(gv, axis=-1, keepdims=True)
    vr = xc * lax.rsqrt(jnp.mean(xc * xc, axis=-1, keepdims=True) + EPS) * gln_ref[...] + bln_ref[...]
    if emit_gv:
        gv_ref[...] = vr
    if chunked:
        vr16 = vr.astype(BF16)
        row = lax.broadcasted_iota(I32, (CHUNK, CHUNK), 0)
        col = lax.broadcasted_iota(I32, (CHUNK, CHUNK), 1)
        for g in range(GMLP_GROUPS):
            w = jnp.where(row >= col, wsp_ref[g], 0.0).astype(BF16)
            gs = slice(g * GMLP_GROUP_DIM, (g + 1) * GMLP_GROUP_DIM)
            for c in range(tm // CHUNK):
                rs = slice(c * CHUNK, (c + 1) * CHUNK)
                sp = jnp.dot(w, vr16[rs, gs], preferred_element_type=F32) + bsp_ref[g]
                sg_scr[rs, gs] = (u[rs, gs] * sp).astype(BF16)
    else:
        sg_scr[...] = (u * (vr * wsp_ref[...] + bsp_ref[...])).astype(BF16)
    yb = jnp.dot(sg_scr[...], wbb_ref[...], preferred_element_type=F32)
    gyb_ref[...] = (jax.nn.sigmoid(proj(COL_GB, D_MODEL)) * yb).astype(BF16)
    sga_ref[...] = jax.nn.sigmoid(proj(COL_GA, D_MODEL)).astype(BF16)


def _inproj(x3, shift, scale, g1, win, rope_tabs, wsp, bsp, gln, bln, wbb, *, tm, chunked, emit_gv):
    B, S, D = x3.shape
    per_token = shift.shape[1] != 1
    nt = S // tm
    mod_spec = (pl.BlockSpec((None, tm, D), lambda b, i: (b, i, 0)) if per_token
                else pl.BlockSpec((None, 1, D), lambda b, i: (b, 0, 0)))
    rope_rows = rope_tabs[0].shape[0]
    rope_spec = (pl.BlockSpec((tm, LANES), lambda b, i: (i, 0)) if rope_rows != 1
                 else pl.BlockSpec((1, LANES), lambda b, i: (0, 0)))
    const2 = lambda a: pl.BlockSpec(a.shape, lambda b, i: (0,) * a.ndim)
    tok = lambda n: pl.BlockSpec((None, tm, n), lambda b, i: (b, i, 0))
    out_shapes = [jax.ShapeDtypeStruct((B, S, Q_COLS), BF16),
                  jax.ShapeDtypeStruct((B, S, K_COLS), F32),
                  jax.ShapeDtypeStruct((B, S, V_COLS), F32),
                  jax.ShapeDtypeStruct((B, S, K_COLS), BF16),
                  jax.ShapeDtypeStruct((B, S, V_COLS), BF16),
                  jax.ShapeDtypeStruct((B, S, D), BF16),
                  jax.ShapeDtypeStruct((B, S, D), BF16)]
    out_specs = [tok(Q_COLS), tok(K_COLS), tok(V_COLS), tok(K_COLS), tok(V_COLS), tok(D), tok(D)]
    if emit_gv:
        out_shapes.append(jax.ShapeDtypeStruct((B, S, D), F32))
        out_specs.append(tok(D))
    return pl.pallas_call(
        functools.partial(_inproj_kernel, chunked=chunked, emit_gv=emit_gv),
        grid=(B, nt),
        in_specs=[tok(D), mod_spec, mod_spec, const2(g1), const2(win),
                  rope_spec, rope_spec, rope_spec, const2(wsp), const2(bsp),
                  const2(gln), const2(bln), const2(wbb)],
        out_specs=out_specs,
        out_shape=out_shapes,
        scratch_shapes=[pltpu.VMEM((tm, D), BF16)],
        compiler_params=_params(("arbitrary", "arbitrary")),
        name="inproj",
    )(x3, shift, scale, g1, win, *rope_tabs, wsp, bsp, gln, bln, wbb)


def _attn_prompt_kernel(lam_ref, q_ref, k_ref, v_ref, gsub_ref, o_ref, m_scr, l_scr, acc_scr,
                        *, tq, out_scale):
    i = pl.program_id(2)
    q = q_ref[...]
    lane = lax.broadcasted_iota(I32, (tq, LANES), 1)
    lo = lane < HEAD_DIM
    zero = jnp.zeros((tq, LANES), BF16)
    heads = [q[:, g * LANES:(g + 1) * LANES] for g in range(GQA)]
    qm = [jnp.concatenate([jnp.where(lo, h, zero) for h in heads], axis=0),
          jnp.concatenate([jnp.where(lo, zero, h) for h in heads], axis=0)]
    rows = GQA * tq
    m_scr[...] = jnp.full(m_scr.shape, NEG_INF, F32)
    l_scr[...] = jnp.zeros(l_scr.shape, F32)
    acc_scr[...] = jnp.zeros(acc_scr.shape, F32)

    def step(j, masked):
        start = pl.multiple_of(j * tq, tq)
        k = k_ref[pl.ds(start, tq), :]
        v = v_ref[pl.ds(start, tq), :]
        for mi in range(2):
            s = lax.dot_general(qm[mi], k, (((1,), (1,)), ((), ())), preferred_element_type=F32)
            if masked:
                r = lax.broadcasted_iota(I32, (rows, tq), 0)
                c = lax.broadcasted_iota(I32, (rows, tq), 1)
                qpos = jnp.where(r >= tq, r - tq, r)
                s = jnp.where(qpos >= c, s, NEG_INF)
            m_prev = m_scr[mi]
            m_new = jnp.maximum(m_prev, jnp.max(s, axis=-1, keepdims=True))
            alpha = jnp.exp(m_prev - m_new)
            p = jnp.exp(s - m_new)
            l_scr[mi] = alpha * l_scr[mi] + jnp.sum(p, axis=-1, keepdims=True)
            acc_scr[mi] = alpha * acc_scr[mi] + jnp.dot(p.astype(BF16), v, preferred_element_type=F32)
            m_scr[mi] = m_new

    def body(j, carry):
        step(j, False)
        return carry

    lax.fori_loop(0, i, body, 0)
    step(i, True)

    lam = lam_ref[0, 0]
    o = acc_scr[0] / l_scr[0] - lam * (acc_scr[1] / l_scr[1])
    o = _rmsnorm(o, gsub_ref[...]) * out_scale
    for g in range(GQA):
        o_ref[:, g * V_DIM:(g + 1) * V_DIM] = o[g * tq:(g + 1) * tq].astype(o_ref.dtype)


def _attn_prompt(lam, q, kb, vb, gsub, *, tq, out_scale):
    B, S, _ = q.shape
    rows = GQA * tq
    return pl.pallas_call(
        functools.partial(_attn_prompt_kernel, tq=tq, out_scale=out_scale),
        grid=(B, N_KV_HEADS, S // tq),
        in_specs=[pl.BlockSpec(memory_space=pltpu.SMEM),
                  pl.BlockSpec((None, tq, GQA * LANES), lambda b, n, i: (b, i, n)),
                  pl.BlockSpec((None, S, LANES), lambda b, n, i: (b, 0, n)),
                  pl.BlockSpec((None, S, LANES), lambda b, n, i: (b, 0, n)),
                  pl.BlockSpec((1, V_DIM), lambda b, n, i: (0, 0))],
        out_specs=pl.BlockSpec((None, tq, GQA * V_DIM), lambda b, n, i: (b, i, n)),
        out_shape=jax.ShapeDtypeStruct((B, S, N_HEADS * V_DIM), BF16),
        scratch_shapes=[pltpu.VMEM((2, rows, 1), F32), pltpu.VMEM((2, rows, 1), F32),
                        pltpu.VMEM((2, rows, V_DIM), F32)],
        compiler_params=_params(("arbitrary", "arbitrary", "arbitrary")),
        name="attn_prompt",
    )(lam, q, kb, vb, gsub)


N_QROWS = 2 * N_HEADS
PAGE_ROWS = PAGE_SIZE * N_KV_HEADS


def _attn_sample_kernel(pt_ref, lam_ref, q_ref, kn_ref, vn_ref, gsub_ref, *rest, pages, out_scale):
    k_refs = rest[:pages]
    v_refs = rest[pages:2 * pages]
    o_ref, m_scr, l_scr, acc_scr = rest[2 * pages:]
    c = pl.program_id(1)

    @pl.when(c == 0)
    def _():
        m_scr[...] = jnp.full(m_scr.shape, NEG_INF, F32)
        l_scr[...] = jnp.zeros(l_scr.shape, F32)
        acc_scr[...] = jnp.zeros(acc_scr.shape, F32)

    q = q_ref[...]
    r = lax.broadcasted_iota(I32, (N_QROWS, PAGE_ROWS), 0)
    col = lax.broadcasted_iota(I32, (N_QROWS, PAGE_ROWS), 1)
    valid = (col & (N_KV_HEADS - 1)) == ((r & (N_HEADS - 1)) >> 1)
    scores = []
    for p in range(pages):
        s = lax.dot_general(q, k_refs[p][...].astype(BF16), (((1,), (1,)), ((), ())),
                            preferred_element_type=F32)
        scores.append(jnp.where(valid, s, NEG_INF))
    m_prev = m_scr[...]
    m_new = m_prev
    for s in scores:
        m_new = jnp.maximum(m_new, jnp.max(s, axis=-1, keepdims=True))
    alpha = jnp.exp(m_prev - m_new)
    l_new = alpha * l_scr[...]
    acc = alpha * acc_scr[...]
    for p in range(pages):
        e = jnp.where(valid, jnp.exp(scores[p] - m_new), 0.0)
        l_new = l_new + jnp.sum(e, axis=-1, keepdims=True)
        acc = acc + jnp.dot(e.astype(BF16), v_refs[p][...].astype(BF16), preferred_element_type=F32)
    m_scr[...] = m_new
    l_scr[...] = l_new
    acc_scr[...] = acc

    @pl.when(c == pl.num_programs(1) - 1)
    def _():
        s_self = jnp.sum(q.astype(F32) * kn_ref[...], axis=-1, keepdims=True)
        m_fin = jnp.maximum(m_new, s_self)
        a = jnp.exp(m_new - m_fin)
        e_self = jnp.exp(s_self - m_fin)
        o = (a * acc + e_self * vn_ref[...]) / (a * l_new + e_self)
        lam = lam_ref[0, 0]
        od = o[:N_HEADS] - lam * o[N_HEADS:]
        o_ref[...] = (_rmsnorm(od, gsub_ref[...]) * out_scale).astype(o_ref.dtype)


def _attn_sample(page_table, lam, q16, kn16, vn16, gsub, ck2, cv2, *, pages, out_scale):
    DB, n_pages = page_table.shape
    steps = n_pages // pages

    def cache_spec(p):
        return pl.BlockSpec((PAGE_ROWS, LANES), lambda b, c, pt: (pt[b, c * pages + p], 0))

    row3 = lambda n: pl.BlockSpec((None, n, LANES), lambda b, c, pt: (b, 0, 0))
    grid_spec = pltpu.PrefetchScalarGridSpec(
        num_scalar_prefetch=1,
        grid=(DB, steps),
        in_specs=[pl.BlockSpec(memory_space=pltpu.SMEM), row3(N_QROWS), row3(N_QROWS), row3(N_QROWS),
                  pl.BlockSpec((1, V_DIM), lambda b, c, pt: (0, 0))]
                 + [cache_spec(p) for p in range(pages)] * 2,
        out_specs=row3(N_HEADS),
        scratch_shapes=[pltpu.VMEM((N_QROWS, 1), F32), pltpu.VMEM((N_QROWS, 1), F32),
                        pltpu.VMEM((N_QROWS, V_DIM), F32)],
    )
    return pl.pallas_call(
        functools.partial(_attn_sample_kernel, pages=pages, out_scale=out_scale),
        grid_spec=grid_spec,
        out_shape=jax.ShapeDtypeStruct((DB, N_HEADS, V_DIM), BF16),
        compiler_params=_params(("arbitrary", "arbitrary")),
        name="attn_sample",
    )(page_table, lam, q16, kn16, vn16, gsub, *([ck2] * pages), *([cv2] * pages))


def _post_kernel(o_ref, sga_ref, gyb_ref, x_ref, gate_ref, shift_ref, scale_ref, g2_ref,
                 wba_ref, wout_ref, wr_ref, br_ref,
                 x1_ref, h2p_ref, route_ref, counts_ref, base_scr):
    tm = x_ref.shape[0]
    first = (pl.program_id(0) == 0) & (pl.program_id(1) == 0)

    @pl.when(first)
    def _():
        base_scr[...] = jnp.zeros(base_scr.shape, F32)

    ya = jnp.dot(o_ref[...], wba_ref[...], preferred_element_type=F32)
    merged = sga_ref[...].astype(F32) * ya + gyb_ref[...].astype(F32)
    upd = jnp.dot(merged.astype(BF16), wout_ref[...], preferred_element_type=F32)
    x1 = x_ref[...] + gate_ref[...] * upd
    x1_ref[...] = x1
    h2 = _rmsnorm(x1, g2_ref[...]) * (1.0 + scale_ref[...]) + shift_ref[...]
    half = D_MODEL // 2
    h2p_ref[...] = _pack_pair(h2[:, :half], h2[:, half:])

    logits = jnp.dot(h2, wr_ref[...], preferred_element_type=F32, precision=HIGHEST) + br_ref[...]
    lane = lax.broadcasted_iota(I32, (tm, LANES), 1)
    lanef = lane.astype(F32)
    big = float(LANES)
    is_g = lane < N_GROUPS
    lg = jnp.where(is_g, logits, -jnp.inf)
    gmax = jnp.max(lg, axis=-1, keepdims=True)
    g_idx = jnp.min(jnp.where(lg == gmax, lanef, big), axis=-1, keepdims=True)
    p_grp = 1.0 / jnp.sum(jnp.where(is_g, jnp.exp(logits - gmax), 0.0), axis=-1, keepdims=True)
    e_lo = N_GROUPS + EXPERTS_PER_GROUP * g_idx
    in_grp = (lanef >= e_lo) & (lanef < e_lo + EXPERTS_PER_GROUP)
    emax = jnp.max(jnp.where(in_grp, logits, -jnp.inf), axis=-1, keepdims=True)
    ee = jnp.where(in_grp, jnp.exp(logits - emax), 0.0)
    pe = ee / jnp.sum(ee, axis=-1, keepdims=True)
    pe_m = jnp.where(in_grp, pe, -1.0)
    v1 = jnp.max(pe_m, axis=-1, keepdims=True)
    i1 = jnp.min(jnp.where(pe_m == v1, lanef, big), axis=-1, keepdims=True)
    pe_m2 = jnp.where(lanef == i1, -1.0, pe_m)
    v2 = jnp.max(pe_m2, axis=-1, keepdims=True)
    i2 = jnp.min(jnp.where(pe_m2 == v2, lanef, big), axis=-1, keepdims=True)
    tsum = v1 + v2
    w1 = p_grp * (v1 / tsum)
    w2 = p_grp * (v2 / tsum)
    e1 = i1 - N_GROUPS
    e2 = i2 - N_GROUPS

    oh1 = (lanef == e1).astype(F32)
    oh2 = (lanef == e2).astype(F32)
    ohs = oh1 + oh2
    rr = lax.broadcasted_iota(I32, (tm, tm), 0)
    cc = lax.broadcasted_iota(I32, (tm, tm), 1)
    tri = jnp.where(rr > cc, 1.0, 0.0).astype(BF16)
    before = jnp.dot(tri, ohs.astype(BF16), preferred_element_type=F32) + base_scr[...]
    r1 = jnp.sum(before * oh1, axis=-1, keepdims=True)
    r2 = jnp.sum(before * oh2, axis=-1, keepdims=True)
    new_base = base_scr[...] + jnp.sum(ohs, axis=0, keepdims=True)
    base_scr[...] = new_base
    counts_ref[...] = jnp.broadcast_to(new_base, counts_ref.shape)

    route = jnp.zeros((tm, LANES), F32)
    for idx, val in enumerate((e1, e2, r1, r2, w1, w2)):
        route = jnp.where(lane == idx, val, route)
    route_ref[...] = route


def _post(o, sga, gyb, x3, gate, shift, scale, g2, wba, wout, wr, br, *, tm):
    B, S, D = x3.shape
    per_token = gate.shape[1] != 1
    mod_spec = (pl.BlockSpec((None, tm, D), lambda b, i: (b, i, 0)) if per_token
                else pl.BlockSpec((None, 1, D), lambda b, i: (b, 0, 0)))
    const2 = lambda a: pl.BlockSpec(a.shape, lambda b, i: (0,) * a.ndim)
    tok = lambda n: pl.BlockSpec((None, tm, n), lambda b, i: (b, i, 0))
    return pl.pallas_call(
        _post_kernel,
        grid=(B, S // tm),
        in_specs=[tok(D), tok(D), tok(D), tok(D), mod_spec, mod_spec, mod_spec, const2(g2),
                  const2(wba), const2(wout), const2(wr), const2(br)],
        out_specs=[tok(D), tok(D // 2), tok(LANES), pl.BlockSpec((8, LANES), lambda b, i: (0, 0))],
        out_shape=[jax.ShapeDtypeStruct((B, S, D), F32),
                   jax.ShapeDtypeStruct((B, S, D // 2), U32),
                   jax.ShapeDtypeStruct((B, S, LANES), F32),
                   jax.ShapeDtypeStruct((8, LANES), F32)],
        scratch_shapes=[pltpu.VMEM((1, LANES), F32)],
        compiler_params=_params(("arbitrary", "arbitrary")),
        name="post",
    )(o, sga, gyb, x3, gate, shift, scale, g2, wba, wout, wr, br)


def _row_copy(src, dst, src_row, dst_row, sem):
    return pltpu.make_async_copy(src.at[pl.ds(src_row, 1)], dst.at[pl.ds(dst_row, 1)], sem)


def _dispatch_kernel(dest_ref, h_ref, xs_in_ref, xs_ref, dest_smem, sem, dsem):
    del xs_in_ref
    tm = h_ref.shape[0]
    cp = pltpu.make_async_copy(dest_ref.at[0], dest_smem, dsem)
    cp.start()
    cp.wait()

    def issue(r, carry):
        _row_copy(h_ref, xs_ref, r, dest_smem[0, 2 * r], sem).start()
        _row_copy(h_ref, xs_ref, r, dest_smem[0, 2 * r + 1], sem).start()
        return carry

    lax.fori_loop(0, tm, issue, 0)

    def drain(r, carry):
        _row_copy(h_ref, xs_ref, 0, 0, sem).wait()
        return carry

    lax.fori_loop(0, 2 * tm, drain, 0)


def _dispatch(dest, h2p, slots, *, tm):
    T, W = h2p.shape
    nt = T // tm
    xs0 = jnp.zeros((slots, W), U32)
    return pl.pallas_call(
        _dispatch_kernel,
        grid=(nt,),
        in_specs=[pl.BlockSpec((1, 1, 2 * tm), lambda i: (i, 0, 0)),
                  pl.BlockSpec((tm, W), lambda i: (i, 0)),
                  pl.BlockSpec(memory_space=pl.ANY)],
        out_specs=pl.BlockSpec(memory_space=pl.ANY),
        out_shape=jax.ShapeDtypeStruct((slots, W), U32),
        scratch_shapes=[pltpu.SMEM((1, 2 * tm), I32), pltpu.SemaphoreType.DMA(()), pltpu.SemaphoreType.DMA(())],
        input_output_aliases={2: 0},
        compiler_params=_params(("arbitrary",)),
        name="dispatch",
    )(dest.reshape(nt, 1, 2 * tm), h2p, xs0)


def _ffn_kernel(be_ref, nu_ref, x_ref, wg_ref, wu_ref, wd_ref, y_ref):
    del be_ref
    i = pl.program_id(0)

    @pl.when(i < nu_ref[0])
    def _():
        a, b = _unpack_pair(x_ref[...])
        x = jnp.concatenate([a.astype(BF16), b.astype(BF16)], axis=1)
        hg = jnp.dot(x, wg_ref[...], preferred_element_type=F32)
        hu = jnp.dot(x, wu_ref[...], preferred_element_type=F32)
        hid = (hg * jax.nn.sigmoid(hg)) * hu
        y = jnp.dot(hid.astype(BF16), wd_ref[...], preferred_element_type=F32)
        half = D_MODEL // 2
        y_ref[...] = _pack_pair(y[:, :half], y[:, half:])

    @pl.when(i >= nu_ref[0])
    def _():
        y_ref[...] = jnp.zeros(y_ref.shape, U32)


def _ffn(block_expert, n_used, xs, weg, weu, wed, *, bm):
    slots, W = xs.shape
    nb = slots // bm
    grid_spec = pltpu.PrefetchScalarGridSpec(
        num_scalar_prefetch=2,
        grid=(nb,),
        in_specs=[pl.BlockSpec((bm, W), lambda i, be, nu: (i, 0)),
                  pl.BlockSpec((None, D_MODEL, D_EXPERT), lambda i, be, nu: (be[i], 0, 0)),
                  pl.BlockSpec((None, D_MODEL, D_EXPERT), lambda i, be, nu: (be[i], 0, 0)),
                  pl.BlockSpec((None, D_EXPERT, D_MODEL), lambda i, be, nu: (be[i], 0, 0))],
        out_specs=pl.BlockSpec((bm, W), lambda i, be, nu: (i, 0)),
    )
    return pl.pallas_call(
        _ffn_kernel,
        grid_spec=grid_spec,
        out_shape=jax.ShapeDtypeStruct((slots, W), U32),
        compiler_params=_params(("arbitrary",)),
        name="ffn",
    )(block_expert, n_used, xs, weg, weu, wed)


def _combine_kernel(dest_ref, ys_ref, x1_ref, route_ref, gate_ref, fshift_ref, fscale_ref, gf_ref,
                    o_ref, dest_smem, rows_scr, sem, dsem):
    tm = x1_ref.shape[0]
    cp = pltpu.make_async_copy(dest_ref.at[0], dest_smem, dsem)
    cp.start()
    cp.wait()

    def issue(r, carry):
        _row_copy(ys_ref, rows_scr.at[0], dest_smem[0, 2 * r], r, sem).start()
        _row_copy(ys_ref, rows_scr.at[1], dest_smem[0, 2 * r + 1], r, sem).start()
        return carry

    lax.fori_loop(0, tm, issue, 0)

    def drain(r, carry):
        _row_copy(ys_ref, rows_scr.at[0], 0, 0, sem).wait()
        return carry

    lax.fori_loop(0, 2 * tm, drain, 0)

    route = route_ref[...]
    w1 = route[:, 4:5]
    w2 = route[:, 5:6]
    a1, b1 = _unpack_pair(rows_scr[0])
    a2, b2 = _unpack_pair(rows_scr[1])
    y = jnp.concatenate([a1 * w1 + a2 * w2, b1 * w1 + b2 * w2], axis=1)
    x2 = x1_ref[...] + gate_ref[...] * y
    o_ref[...] = _rmsnorm(x2, gf_ref[...]) * (1.0 + fscale_ref[...]) + fshift_ref[...]


def _combine(dest, ys, x1, route, gate, fshift, fscale, gf, *, tm):
    B, S, D = x1.shape
    nt = S // tm
    W = ys.shape[1]
    per_token = gate.shape[1] != 1
    mod_spec = (pl.BlockSpec((None, tm, D), lambda b, i: (b, i, 0)) if per_token
                else pl.BlockSpec((None, 1, D), lambda b, i: (b, 0, 0)))
    tok = lambda n: pl.BlockSpec((None, tm, n), lambda b, i: (b, i, 0))
    return pl.pallas_call(
        _combine_kernel,
        grid=(B, nt),
        in_specs=[pl.BlockSpec((1, 1, 2 * tm), lambda b, i: (b * nt + i, 0, 0)),
                  pl.BlockSpec(memory_space=pl.ANY),
                  tok(D), tok(LANES), mod_spec, mod_spec, mod_spec,
                  pl.BlockSpec(gf.shape, lambda b, i: (0, 0))],
        out_specs=tok(D),
        out_shape=jax.ShapeDtypeStruct((B, S, D), F32),
        scratch_shapes=[pltpu.SMEM((1, 2 * tm), I32), pltpu.VMEM((2, tm, W), U32),
                        pltpu.SemaphoreType.DMA(()), pltpu.SemaphoreType.DMA(())],
        compiler_params=_params(("arbitrary", "arbitrary")),
        name="combine",
    )(dest.reshape(B * nt, 1, 2 * tm), ys, x1, route, gate, fshift, fscale, gf)


def _rope_tables(pos):
    half = ROPE_DIM // 2
    inv_freq = ROPE_THETA ** (-jnp.arange(0, ROPE_DIM, 2, dtype=F32) / ROPE_DIM)
    ang = pos.astype(F32)[:, None] * inv_freq[None, :]
    cos, sin = jnp.cos(ang), jnp.sin(ang)
    n = pos.shape[0]
    rest = HEAD_DIM - ROPE_DIM
    c64 = jnp.concatenate([cos, cos, jnp.ones((n, rest), F32)], axis=1)
    a64 = jnp.concatenate([-sin, jnp.zeros((n, HEAD_DIM - half), F32)], axis=1)
    b64 = jnp.concatenate([jnp.zeros((n, half), F32), sin, jnp.zeros((n, rest), F32)], axis=1)
    return tuple(jnp.tile(t, (1, LANES // HEAD_DIM)) for t in (c64, a64, b64))


def _moe_plan(route, counts, bm):
    T = route.shape[0]
    e = route[:, 0:2].astype(I32)
    rank = route[:, 2:4].astype(I32)
    cnt = counts[0, :N_EXPERTS].astype(I32)
    padded = (cnt + bm - 1) // bm * bm
    pad_end = jnp.cumsum(padded)
    pad_start = pad_end - padded
    dest = pad_start[e] + rank
    n_blocks = -(-(2 * T) // bm) + N_EXPERTS
    block_expert = jnp.clip(jnp.searchsorted(pad_end, jnp.arange(n_blocks, dtype=I32) * bm, side='right'),
                            0, N_EXPERTS - 1).astype(I32)
    n_used = (pad_end[-1:] // bm).astype(I32)
    return dest.reshape(-1), block_expert, n_used, n_blocks * bm


def _layer(x3, mod, modf, pos_tabs, attend, lam_init, wts, *, tm, bm, chunked, emit_gv):
    B, S, D = x3.shape
    m = lambda k: mod[:, :, k, :]
    outs = _inproj(x3, m(0), m(1), wts['g1'], wts['win'], pos_tabs, wts['wsp'], wts['bsp'],
                   wts['gln'], wts['bln'], wts['wbb'], tm=tm, chunked=chunked, emit_gv=emit_gv)
    q, kf, vf, kb, vb, sga, gyb = outs[:7]
    gv = outs[7] if emit_gv else None
    o = attend(q, kf, vf, kb, vb)
    x1, h2p, route, counts = _post(o, sga, gyb, x3, m(2), m(3), m(4), wts['g2'], wts['wba'],
                                   wts['wout'], wts['wr'], wts['br'], tm=tm)
    T = B * S
    dest, block_expert, n_used, slots = _moe_plan(route.reshape(T, LANES), counts, bm)
    xs = _dispatch(dest, h2p.reshape(T, D // 2), slots, tm=tm)
    ys = _ffn(block_expert, n_used, xs, wts['weg'], wts['weu'], wts['wed'], bm=bm)
    y = _combine(dest, ys, x1, route, m(5), modf[:, :, 0, :], modf[:, :, 1, :], wts['gf'], tm=tm)
    return y, kf, vf, gv


def kernel(x_prompt, x_sample, cache_k, cache_v, page_table, c_prompt, c_sample, w_ada, b_ada, w_ada_final, b_ada_final, g_norm1, g_norm2, g_final, w_in, lambda_q1, lambda_k1, lambda_q2, lambda_k2, g_subln, g_gmlp_ln, b_gmlp_ln, w_spatial, b_spatial, w_branch_a, w_branch_b, w_out, w_router_group, b_router_group, w_router_expert, b_router_expert, w_exp_gate, w_exp_up, w_exp_down):
    depth = w_ada.shape[0]
    assert depth == 1
    B, S, D = x_prompt.shape
    DB, DS, _ = x_sample.shape
    assert DS == 1 and D == D_MODEL
    n_pages = page_table.shape[1]
    n_past = n_pages * PAGE_SIZE
    l = 0
    lam_init = 0.8 - 0.6 * math.exp(-0.3 * l)
    out_scale = 1.0 - lam_init

    c_all = jnp.concatenate([c_prompt, c_sample], axis=0)
    mod_all = _adaln(c_all, w_ada[l], b_ada[l])
    modf_all = _adaln(c_all, w_ada_final, b_ada_final)
    mod_p = mod_all[:B].reshape(B, 1, 6, D)
    mod_s = mod_all[B:].reshape(1, DB, 6, D)
    modf_p = modf_all[:B].reshape(B, 1, 2, D)
    modf_s = modf_all[B:].reshape(1, DB, 2, D)
    lam = _lam(lambda_q1[l], lambda_k1[l], lambda_q2[l], lambda_k2[l], lam_init)[0:1, 0:1]

    row = lambda a: a.reshape(1, -1)
    wr = jnp.zeros((D, LANES), F32)
    wr = wr.at[:, :N_GROUPS].set(w_router_group[l]).at[:, N_GROUPS:N_GROUPS + N_EXPERTS].set(w_router_expert[l])
    br = jnp.zeros((1, LANES), F32)
    br = br.at[0, :N_GROUPS].set(b_router_group[l]).at[0, N_GROUPS:N_GROUPS + N_EXPERTS].set(b_router_expert[l])
    wts = dict(
        g1=row(g_norm1[l]), g2=row(g_norm2[l]), gf=row(g_final), win=w_in[l].astype(BF16),
        gln=row(g_gmlp_ln[l]), bln=row(b_gmlp_ln[l]), wbb=w_branch_b[l].astype(BF16),
        wba=w_branch_a[l].astype(BF16), wout=w_out[l].astype(BF16), wr=wr, br=br,
        weg=w_exp_gate[l].astype(BF16), weu=w_exp_up[l].astype(BF16), wed=w_exp_down[l].astype(BF16))
    gsub = row(g_subln[l])

    wts_s = dict(wts, wsp=row(jnp.repeat(w_spatial[l][:, 0, 0], GMLP_GROUP_DIM)),
                 bsp=row(jnp.repeat(b_spatial[l][:, 0], GMLP_GROUP_DIM)))
    tabs_s = _rope_tables(jnp.full((1,), n_past, dtype=I32))
    ck2 = cache_k[l].reshape(-1, LANES)
    cv2 = cache_v[l].reshape(-1, LANES)
    half_mask = (jnp.arange(LANES)[None, :] // HEAD_DIM == jnp.arange(2)[:, None])

    def attend_s(q, kf, vf, kb, vb):
        del kb, vb
        qh = q.reshape(DB, 1, N_HEADS, LANES)
        q16 = jnp.where(half_mask[None, :, None, :], qh, jnp.zeros_like(qh)).reshape(DB, N_QROWS, LANES)
        kv_of_row = (jnp.arange(N_QROWS) % N_HEADS) // GQA
        kn16 = kf.reshape(DB, N_KV_HEADS, LANES)[:, kv_of_row]
        vn16 = vf.reshape(DB, N_KV_HEADS, V_DIM)[:, kv_of_row]
        pages = 8 if n_pages % 8 == 0 else (4 if n_pages % 4 == 0 else 1)
        o = _attn_sample(page_table, lam, q16, kn16, vn16, gsub, ck2, cv2, pages=pages, out_scale=out_scale)
        return o.reshape(1, DB, N_HEADS * V_DIM)

    y_s, k_s, v_s, gv_s = _layer(x_sample.reshape(1, DB, D), mod_s, modf_s, tabs_s, attend_s, lam_init,
                                 wts_s, tm=DB, bm=128, chunked=False, emit_gv=True)

    wts_p = dict(wts, wsp=w_spatial[l],
                 bsp=jnp.broadcast_to(b_spatial[l][:, :, None], (GMLP_GROUPS, CHUNK, GMLP_GROUP_DIM)))
    tabs_p = _rope_tables(jnp.arange(S, dtype=I32))
    tq = min(256, S)
    attend_p = lambda q, kf, vf, kb, vb: _attn_prompt(lam, q, kb, vb, gsub, tq=tq, out_scale=out_scale)
    y_p, k_p, v_p, _ = _layer(x_prompt, mod_p, modf_p, tabs_p, attend_p, lam_init, wts_p,
                              tm=min(256, S), bm=512, chunked=True, emit_gv=False)

    return (y_p,
            y_s.reshape(DB, 1, D),
            k_p.reshape(1, B, S, N_KV_HEADS, 2 * HEAD_DIM),
            v_p.reshape(1, B, S, N_KV_HEADS, V_DIM),
            k_s.reshape(1, DB, 1, N_KV_HEADS, 2 * HEAD_DIM),
            v_s.reshape(1, DB, 1, N_KV_HEADS, V_DIM),
            gv_s.reshape(1, DB, 1, D))
```

```python
import functools
import math

import jax
import jax.numpy as jnp
from jax import lax
from jax.experimental import pallas as pl
from jax.experimental.pallas import tpu as pltpu

F32 = jnp.float32
BF16 = jnp.bfloat16
U32 = jnp.uint32
I32 = jnp.int32

D_MODEL = 1024
N_HEADS = 8
N_KV_HEADS = 4
GQA = N_HEADS // N_KV_HEADS
HEAD_DIM = 64
V_DIM = 128
ROPE_DIM = 16
ROPE_THETA = 500000.0
PAGE_SIZE = 128
GMLP_GROUPS = 8
GMLP_GROUP_DIM = 128
CHUNK = 128
N_GROUPS = 4
EXPERTS_PER_GROUP = 8
N_EXPERTS = 32
D_EXPERT = 512
Q_COLS = 1024
K_COLS = 512
V_COLS = 512
COL_Q, COL_K, COL_V, COL_GU, COL_GV, COL_GA, COL_GB = 0, 1024, 1536, 2048, 3072, 4096, 5120
EPS = 1e-6
NEG_INF = -1e30
LANES = 128
VMEM_LIMIT = 56 * 1024 * 1024

HIGHEST = lax.Precision.HIGHEST
Q_SCALE = HEAD_DIM ** -0.5 * math.log2(math.e)


def _params(sem):
    return pltpu.CompilerParams(dimension_semantics=sem, vmem_limit_bytes=VMEM_LIMIT)


def _rmsnorm(x, g):
    return x * lax.rsqrt(jnp.mean(x * x, axis=-1, keepdims=True) + EPS) * g


def _pack_pair(a, b):
    ua = lax.bitcast_convert_type(a.astype(BF16).astype(F32), U32)
    ub = lax.bitcast_convert_type(b.astype(BF16).astype(F32), U32)
    return (ua >> 16) | (ub & jnp.uint32(0xFFFF0000))


def _unpack_pair(w):
    a = lax.bitcast_convert_type(w << 16, F32)
    b = lax.bitcast_convert_type(w & jnp.uint32(0xFFFF0000), F32)
    return a, b


def _adaln_kernel(c_ref, w_ref, b_ref, o_ref):
    c = c_ref[...]
    a = c * jax.nn.sigmoid(c)
    o_ref[...] = jnp.dot(a, w_ref[...], preferred_element_type=F32, precision=HIGHEST) + b_ref[...]


def _adaln(c, w, b, tn=512):
    rows, d = c.shape
    n = w.shape[1]
    return pl.pallas_call(
        _adaln_kernel,
        grid=(n // tn,),
        in_specs=[pl.BlockSpec((rows, d), lambda j: (0, 0)),
                  pl.BlockSpec((d, tn), lambda j: (0, j)),
                  pl.BlockSpec((1, tn), lambda j: (0, j))],
        out_specs=pl.BlockSpec((rows, tn), lambda j: (0, j)),
        out_shape=jax.ShapeDtypeStruct((rows, n), F32),
        compiler_params=_params(("arbitrary",)),
        name="adaln",
    )(c, w, b.reshape(1, n))


def _lam_kernel(q1_ref, k1_ref, q2_ref, k2_ref, o_ref, *, lam_init):
    a = jnp.sum(q1_ref[...] * k1_ref[...], axis=-1, keepdims=True)
    b = jnp.sum(q2_ref[...] * k2_ref[...], axis=-1, keepdims=True)
    lam = jnp.exp(a) - jnp.exp(b) + lam_init
    o_ref[...] = jnp.broadcast_to(lam, o_ref.shape)


def _lam(q1, k1, q2, k2, lam_init):
    args = [a.reshape(1, HEAD_DIM) for a in (q1, k1, q2, k2)]
    return pl.pallas_call(
        functools.partial(_lam_kernel, lam_init=lam_init),
        out_shape=jax.ShapeDtypeStruct((8, LANES), F32),
        name="lam",
    )(*args)


def _inproj_kernel(x_ref, shift_ref, scale_ref, g1_ref, win_ref, rc_ref, ra_ref, rb_ref,
                   wsp_ref, bsp_ref, gln_ref, bln_ref, wbb_ref,
                   q_ref, kf_ref, vf_ref, kb_ref, vt_ref, sga_ref, gyb_ref, *rest,
                   chunked, emit_gv):
    if emit_gv:
        gv_ref, sg_scr = rest
    else:
        (sg_scr,) = rest
    tm = x_ref.shape[0]
    x = x_ref[...]
    h = _rmsnorm(x, g1_ref[...]) * (1.0 + scale_ref[...]) + shift_ref[...]
    hb = h.astype(BF16)

    def proj(c0, n):
        return jnp.dot(hb, win_ref[:, c0:c0 + n], preferred_element_type=F32)

    rc, ra, rb = rc_ref[...], ra_ref[...], rb_ref[...]

    def rope(z):
        return z * rc + pltpu.roll(z, LANES - ROPE_DIM // 2, 1) * ra + pltpu.roll(z, ROPE_DIM // 2, 1) * rb

    zq = proj(COL_Q, Q_COLS)
    for j in range(Q_COLS // LANES):
        z = rope(zq[:, j * LANES:(j + 1) * LANES])
        q_ref[:, j * LANES:(j + 1) * LANES] = (z * Q_SCALE).astype(BF16)
    zk = proj(COL_K, K_COLS)
    for j in range(N_KV_HEADS):
        z = rope(zk[:, j * LANES:(j + 1) * LANES])
        kf_ref[pl.ds(j, tm, stride=N_KV_HEADS), :] = z
        kb_ref[:, j * LANES:(j + 1) * LANES] = z.astype(BF16)
    zv = proj(COL_V, V_COLS)
    for j in range(N_KV_HEADS):
        vf_ref[pl.ds(j, tm, stride=N_KV_HEADS), :] = zv[:, j * V_DIM:(j + 1) * V_DIM]
    vt_ref[...] = zv.T.astype(BF16)

    u = jax.nn.gelu(proj(COL_GU, D_MODEL))
    gv = jax.nn.gelu(proj(COL_GV, D_MODEL))
    xc = gv - jnp.mean(gv, axis=-1, keepdims=True)
    vr = xc * lax.rsqrt(jnp.mean(xc * xc, axis=-1, keepdims=True) + EPS) * gln_ref[...] + bln_ref[...]
    if emit_gv:
        gv_ref[...] = vr
    if chunked:
        vr16 = vr.astype(BF16)
        row = lax.broadcasted_iota(I32, (CHUNK, CHUNK), 0)
        col = lax.broadcasted_iota(I32, (CHUNK, CHUNK), 1)
        for g in range(GMLP_GROUPS):
            w = jnp.where(row >= col, wsp_ref[g], 0.0).astype(BF16)
            gs = slice(g * GMLP_GROUP_DIM, (g + 1) * GMLP_GROUP_DIM)
            for c in range(tm // CHUNK):
                rs = slice(c * CHUNK, (c + 1) * CHUNK)
                sp = jnp.dot(w, vr16[rs, gs], preferred_element_type=F32) + bsp_ref[g]
                sg_scr[rs, gs] = (u[rs, gs] * sp).astype(BF16)
    else:
        sg_scr[...] = (u * (vr * wsp_ref[...] + bsp_ref[...])).astype(BF16)
    yb = jnp.dot(sg_scr[...], wbb_ref[...], preferred_element_type=F32)
    gyb_ref[...] = (jax.nn.sigmoid(proj(COL_GB, D_MODEL)) * yb).astype(BF16)
    sga_ref[...] = jax.nn.sigmoid(proj(COL_GA, D_MODEL)).astype(BF16)


def _inproj(x3, shift, scale, g1, win, rope_tabs, wsp, bsp, gln, bln, wbb, *, tm, chunked, emit_gv):
    B, S, D = x3.shape
    per_token = shift.shape[1] != 1
    nt = S // tm
    mod_spec = (pl.BlockSpec((None, tm, D), lambda b, i: (b, i, 0)) if per_token
                else pl.BlockSpec((None, 1, D), lambda b, i: (b, 0, 0)))
    rope_rows = rope_tabs[0].shape[0]
    rope_spec = (pl.BlockSpec((tm, LANES), lambda b, i: (i, 0)) if rope_rows != 1
                 else pl.BlockSpec((1, LANES), lambda b, i: (0, 0)))
    const2 = lambda a: pl.BlockSpec(a.shape, lambda b, i: (0,) * a.ndim, pipeline_mode=pl.Buffered(1))
    tok = lambda n: pl.BlockSpec((None, tm, n), lambda b, i: (b, i, 0))
    head_rows = pl.BlockSpec((None, tm * N_KV_HEADS, LANES), lambda b, i: (b, i, 0))
    out_shapes = [jax.ShapeDtypeStruct((B, S, Q_COLS), BF16),
                  jax.ShapeDtypeStruct((B, S * N_KV_HEADS, LANES), F32),
                  jax.ShapeDtypeStruct((B, S * N_KV_HEADS, V_DIM), F32),
                  jax.ShapeDtypeStruct((B, S, K_COLS), BF16),
                  jax.ShapeDtypeStruct((B, V_COLS, S), BF16),
                  jax.ShapeDtypeStruct((B, S, D), BF16),
                  jax.ShapeDtypeStruct((B, S, D), BF16)]
    out_specs = [tok(Q_COLS), head_rows, head_rows, tok(K_COLS),
                 pl.BlockSpec((None, V_COLS, tm), lambda b, i: (b, 0, i)), tok(D), tok(D)]
    if emit_gv:
        out_shapes.append(jax.ShapeDtypeStruct((B, S, D), F32))
        out_specs.append(tok(D))
    return pl.pallas_call(
        functools.partial(_inproj_kernel, chunked=chunked, emit_gv=emit_gv),
        grid=(B, nt),
        in_specs=[tok(D), mod_spec, mod_spec, const2(g1), const2(win),
                  rope_spec, rope_spec, rope_spec, const2(wsp), const2(bsp),
                  const2(gln), const2(bln), const2(wbb)],
        out_specs=out_specs,
        out_shape=out_shapes,
        scratch_shapes=[pltpu.VMEM((tm, D), BF16)],
        compiler_params=_params(("arbitrary", "arbitrary")),
        name="inproj",
    )(x3, shift, scale, g1, win, *rope_tabs, wsp, bsp, gln, bln, wbb)


def _attn_prompt_kernel(lam_ref, q_ref, k_ref, vt_ref, gsub_ref, o_ref, m_scr, l_scr, acc_scr,
                        *, tq, out_scale):
    i = pl.program_id(2)
    q = q_ref[...]
    lane = lax.broadcasted_iota(I32, (tq, LANES), 1)
    lo = lane < HEAD_DIM
    zero = jnp.zeros((tq, LANES), BF16)
    heads = [q[:, g * LANES:(g + 1) * LANES] for g in range(GQA)]
    qm = [jnp.concatenate([jnp.where(lo, h, zero) for h in heads], axis=0),
          jnp.concatenate([jnp.where(lo, zero, h) for h in heads], axis=0)]
    cols = GQA * tq
    m_scr[...] = jnp.full(m_scr.shape, NEG_INF, F32)
    l_scr[...] = jnp.zeros(l_scr.shape, F32)
    acc_scr[...] = jnp.zeros(acc_scr.shape, F32)

    def step(j, masked):
        start = pl.multiple_of(j * tq, tq)
        k = k_ref[pl.ds(start, tq), :]
        vt = vt_ref[:, pl.ds(start, tq)]
        for mi in range(2):
            s = lax.dot_general(k, qm[mi], (((1,), (1,)), ((), ())), preferred_element_type=F32)
            if masked:
                kpos = lax.broadcasted_iota(I32, (tq, cols), 0)
                c = lax.broadcasted_iota(I32, (tq, cols), 1)
                qpos = jnp.where(c >= tq, c - tq, c)
                s = jnp.where(qpos >= kpos, s, NEG_INF)
            m_prev = m_scr[mi]
            m_new = jnp.maximum(m_prev, jnp.max(s, axis=0, keepdims=True))
            alpha = jnp.exp2(m_prev - m_new)
            p = jnp.exp2(s - m_new)
            l_scr[mi] = alpha * l_scr[mi] + jnp.sum(p, axis=0, keepdims=True)
            acc_scr[mi] = alpha * acc_scr[mi] + jnp.dot(vt, p.astype(BF16), preferred_element_type=F32)
            m_scr[mi] = m_new

    def body(j, carry):
        step(j, False)
        return carry

    lax.fori_loop(0, i, body, 0)
    step(i, True)

    lam = lam_ref[0, 0]
    ot = acc_scr[0] / l_scr[0] - lam * (acc_scr[1] / l_scr[1])
    ot = ot * lax.rsqrt(jnp.mean(ot * ot, axis=0, keepdims=True) + EPS) * gsub_ref[...] * out_scale
    o = ot.T
    for g in range(GQA):
        o_ref[:, g * V_DIM:(g + 1) * V_DIM] = o[g * tq:(g + 1) * tq].astype(o_ref.dtype)


def _attn_prompt(lam, q, kb, vt, gsub_col, *, tq, out_scale):
    B, S, _ = q.shape
    cols = GQA * tq
    return pl.pallas_call(
        functools.partial(_attn_prompt_kernel, tq=tq, out_scale=out_scale),
        grid=(B, N_KV_HEADS, S // tq),
        in_specs=[pl.BlockSpec(memory_space=pltpu.SMEM),
                  pl.BlockSpec((None, tq, GQA * LANES), lambda b, n, i: (b, i, n)),
                  pl.BlockSpec((None, S, LANES), lambda b, n, i: (b, 0, n)),
                  pl.BlockSpec((None, V_DIM, S), lambda b, n, i: (b, n, 0)),
                  pl.BlockSpec((V_DIM, 1), lambda b, n, i: (0, 0))],
        out_specs=pl.BlockSpec((None, tq, GQA * V_DIM), lambda b, n, i: (b, i, n)),
        out_shape=jax.ShapeDtypeStruct((B, S, N_HEADS * V_DIM), BF16),
        scratch_shapes=[pltpu.VMEM((2, 1, cols), F32), pltpu.VMEM((2, 1, cols), F32),
                        pltpu.VMEM((2, V_DIM, cols), F32)],
        compiler_params=_params(("arbitrary", "arbitrary", "arbitrary")),
        name="attn_prompt",
    )(lam, q, kb, vt, gsub_col)


N_QROWS = 2 * N_HEADS
PAGE_ROWS = PAGE_SIZE * N_KV_HEADS


def _attn_sample_kernel(pt_ref, lam_ref, q_ref, kn_ref, vn_ref, gsub_ref, *rest, pages, out_scale):
    k_refs = rest[:pages]
    v_refs = rest[pages:2 * pages]
    o_ref, m_scr, l_scr, acc_scr = rest[2 * pages:]
    c = pl.program_id(1)

    @pl.when(c == 0)
    def _():
        m_scr[...] = jnp.full(m_scr.shape, NEG_INF, F32)
        l_scr[...] = jnp.zeros(l_scr.shape, F32)
        acc_scr[...] = jnp.zeros(acc_scr.shape, F32)

    q = q_ref[...]
    r = lax.broadcasted_iota(I32, (N_QROWS, PAGE_ROWS), 0)
    col = lax.broadcasted_iota(I32, (N_QROWS, PAGE_ROWS), 1)
    valid = (col & (N_KV_HEADS - 1)) == ((r & (N_HEADS - 1)) >> 1)
    scores = []
    for p in range(pages):
        s = lax.dot_general(q, k_refs[p][...].astype(BF16), (((1,), (1,)), ((), ())),
                            preferred_element_type=F32)
        scores.append(jnp.where(valid, s, NEG_INF))
    m_prev = m_scr[...]
    m_new = m_prev
    for s in scores:
        m_new = jnp.maximum(m_new, jnp.max(s, axis=-1, keepdims=True))
    alpha = jnp.exp2(m_prev - m_new)
    l_new = alpha * l_scr[...]
    acc = alpha * acc_scr[...]
    for p in range(pages):
        e = jnp.where(valid, jnp.exp2(scores[p] - m_new), 0.0)
        l_new = l_new + jnp.sum(e, axis=-1, keepdims=True)
        acc = acc + jnp.dot(e.astype(BF16), v_refs[p][...].astype(BF16), preferred_element_type=F32)
    m_scr[...] = m_new
    l_scr[...] = l_new
    acc_scr[...] = acc

    @pl.when(c == pl.num_programs(1) - 1)
    def _():
        s_self = jnp.sum(q.astype(F32) * kn_ref[...], axis=-1, keepdims=True)
        m_fin = jnp.maximum(m_new, s_self)
        a = jnp.exp2(m_new - m_fin)
        e_self = jnp.exp2(s_self - m_fin)
        o = (a * acc + e_self * vn_ref[...]) / (a * l_new + e_self)
        lam = lam_ref[0, 0]
        od = o[:N_HEADS] - lam * o[N_HEADS:]
        o_ref[...] = (_rmsnorm(od, gsub_ref[...]) * out_scale).astype(o_ref.dtype)


def _attn_sample(page_table, lam, q16, kn16, vn16, gsub, ck2, cv2, *, pages, out_scale):
    DB, n_pages = page_table.shape
    steps = n_pages // pages

    def cache_spec(p):
        return pl.BlockSpec((PAGE_ROWS, LANES), lambda b, c, pt: (pt[b, c * pages + p], 0))

    row3 = lambda n: pl.BlockSpec((None, n, LANES), lambda b, c, pt: (b, 0, 0))
    grid_spec = pltpu.PrefetchScalarGridSpec(
        num_scalar_prefetch=1,
        grid=(DB, steps),
        in_specs=[pl.BlockSpec(memory_space=pltpu.SMEM), row3(N_QROWS), row3(N_QROWS), row3(N_QROWS),
                  pl.BlockSpec((1, V_DIM), lambda b, c, pt: (0, 0))]
                 + [cache_spec(p) for p in range(pages)] * 2,
        out_specs=row3(N_HEADS),
        scratch_shapes=[pltpu.VMEM((N_QROWS, 1), F32), pltpu.VMEM((N_QROWS, 1), F32),
                        pltpu.VMEM((N_QROWS, V_DIM), F32)],
    )
    return pl.pallas_call(
        functools.partial(_attn_sample_kernel, pages=pages, out_scale=out_scale),
        grid_spec=grid_spec,
        out_shape=jax.ShapeDtypeStruct((DB, N_HEADS, V_DIM), BF16),
        compiler_params=_params(("arbitrary", "arbitrary")),
        name="attn_sample",
    )(page_table, lam, q16, kn16, vn16, gsub, *([ck2] * pages), *([cv2] * pages))


def _post_kernel(o_ref, sga_ref, gyb_ref, x_ref, gate_ref, shift_ref, scale_ref, g2_ref,
                 wba_ref, wout_ref, wr_ref, br_ref,
                 x1_ref, h2p_ref, route_ref, counts_ref, base_scr):
    tm = x_ref.shape[0]
    first = (pl.program_id(0) == 0) & (pl.program_id(1) == 0)

    @pl.when(first)
    def _():
        base_scr[...] = jnp.zeros(base_scr.shape, F32)

    ya = jnp.dot(o_ref[...], wba_ref[...], preferred_element_type=F32)
    merged = sga_ref[...].astype(F32) * ya + gyb_ref[...].astype(F32)
    upd = jnp.dot(merged.astype(BF16), wout_ref[...], preferred_element_type=F32)
    x1 = x_ref[...] + gate_ref[...] * upd
    x1_ref[...] = x1
    h2 = _rmsnorm(x1, g2_ref[...]) * (1.0 + scale_ref[...]) + shift_ref[...]
    half = D_MODEL // 2
    h2p_ref[...] = _pack_pair(h2[:, :half], h2[:, half:])

    logits = jnp.dot(h2, wr_ref[...], preferred_element_type=F32, precision=HIGHEST) + br_ref[...]
    lane = lax.broadcasted_iota(I32, (tm, LANES), 1)
    lanef = lane.astype(F32)
    big = float(LANES)
    is_g = lane < N_GROUPS
    lg = jnp.where(is_g, logits, -jnp.inf)
    gmax = jnp.max(lg, axis=-1, keepdims=True)
    g_idx = jnp.min(jnp.where(lg == gmax, lanef, big), axis=-1, keepdims=True)
    p_grp = 1.0 / jnp.sum(jnp.where(is_g, jnp.exp(logits - gmax), 0.0), axis=-1, keepdims=True)
    e_lo = N_GROUPS + EXPERTS_PER_GROUP * g_idx
    in_grp = (lanef >= e_lo) & (lanef < e_lo + EXPERTS_PER_GROUP)
    emax = jnp.max(jnp.where(in_grp, logits, -jnp.inf), axis=-1, keepdims=True)
    ee = jnp.where(in_grp, jnp.exp(logits - emax), 0.0)
    pe = ee / jnp.sum(ee, axis=-1, keepdims=True)
    pe_m = jnp.where(in_grp, pe, -1.0)
    v1 = jnp.max(pe_m, axis=-1, keepdims=True)
    i1 = jnp.min(jnp.where(pe_m == v1, lanef, big), axis=-1, keepdims=True)
    pe_m2 = jnp.where(lanef == i1, -1.0, pe_m)
    v2 = jnp.max(pe_m2, axis=-1, keepdims=True)
    i2 = jnp.min(jnp.where(pe_m2 == v2, lanef, big), axis=-1, keepdims=True)
    tsum = v1 + v2
    w1 = p_grp * (v1 / tsum)
    w2 = p_grp * (v2 / tsum)
    e1 = i1 - N_GROUPS
    e2 = i2 - N_GROUPS

    oh1 = (lanef == e1).astype(F32)
    oh2 = (lanef == e2).astype(F32)
    ohs = oh1 + oh2
    rr = lax.broadcasted_iota(I32, (tm, tm), 0)
    cc = lax.broadcasted_iota(I32, (tm, tm), 1)
    tri = jnp.where(rr > cc, 1.0, 0.0).astype(BF16)
    before = jnp.dot(tri, ohs.astype(BF16), preferred_element_type=F32) + base_scr[...]
    r1 = jnp.sum(before * oh1, axis=-1, keepdims=True)
    r2 = jnp.sum(before * oh2, axis=-1, keepdims=True)
    new_base = base_scr[...] + jnp.sum(ohs, axis=0, keepdims=True)
    base_scr[...] = new_base
    counts_ref[...] = jnp.broadcast_to(new_base, counts_ref.shape)

    route = jnp.zeros((tm, LANES), F32)
    for idx, val in enumerate((e1, e2, r1, r2, w1, w2)):
        route = jnp.where(lane == idx, val, route)
    route_ref[...] = route


def _post(o, sga, gyb, x3, gate, shift, scale, g2, wba, wout, wr, br, *, tm):
    B, S, D = x3.shape
    per_token = gate.shape[1] != 1
    mod_spec = (pl.BlockSpec((None, tm, D), lambda b, i: (b, i, 0)) if per_token
                else pl.BlockSpec((None, 1, D), lambda b, i: (b, 0, 0)))
    const2 = lambda a: pl.BlockSpec(a.shape, lambda b, i: (0,) * a.ndim, pipeline_mode=pl.Buffered(1))
    tok = lambda n: pl.BlockSpec((None, tm, n), lambda b, i: (b, i, 0))
    return pl.pallas_call(
        _post_kernel,
        grid=(B, S // tm),
        in_specs=[tok(D), tok(D), tok(D), tok(D), mod_spec, mod_spec, mod_spec, const2(g2),
                  const2(wba), const2(wout), const2(wr), const2(br)],
        out_specs=[tok(D), tok(D // 2), tok(LANES), pl.BlockSpec((8, LANES), lambda b, i: (0, 0))],
        out_shape=[jax.ShapeDtypeStruct((B, S, D), F32),
                   jax.ShapeDtypeStruct((B, S, D // 2), U32),
                   jax.ShapeDtypeStruct((B, S, LANES), F32),
                   jax.ShapeDtypeStruct((8, LANES), F32)],
        scratch_shapes=[pltpu.VMEM((1, LANES), F32)],
        compiler_params=_params(("arbitrary", "arbitrary")),
        name="post",
    )(o, sga, gyb, x3, gate, shift, scale, g2, wba, wout, wr, br)


ROW_UNROLL = 8


def _row_copy(src, dst, src_row, dst_row, sem):
    return pltpu.make_async_copy(src.at[pl.ds(src_row, 1)], dst.at[pl.ds(dst_row, 1)], sem)


def _dispatch_kernel(dest_ref, h_ref, xs_in_ref, xs_ref, dest_smem, sem, dsem):
    del xs_in_ref
    tm = h_ref.shape[0]
    cp = pltpu.make_async_copy(dest_ref.at[0], dest_smem, dsem)
    cp.start()
    cp.wait()

    def issue(blk, carry):
        for u in range(ROW_UNROLL):
            r = blk * ROW_UNROLL + u
            _row_copy(h_ref, xs_ref, r, dest_smem[0, 2 * r], sem).start(priority=0)
            _row_copy(h_ref, xs_ref, r, dest_smem[0, 2 * r + 1], sem).start(priority=1)
        return carry

    lax.fori_loop(0, tm // ROW_UNROLL, issue, 0)
    for _ in range(2):
        pltpu.make_async_copy(h_ref, xs_ref.at[pl.ds(0, tm)], sem).wait()


def _dispatch(dest, h2p, slots, *, tm):
    T, W = h2p.shape
    nt = T // tm
    xs0 = jnp.zeros((slots, W), U32)
    return pl.pallas_call(
        _dispatch_kernel,
        grid=(nt,),
        in_specs=[pl.BlockSpec((1, 1, 2 * tm), lambda i: (i, 0, 0)),
                  pl.BlockSpec((tm, W), lambda i: (i, 0)),
                  pl.BlockSpec(memory_space=pl.ANY)],
        out_specs=pl.BlockSpec(memory_space=pl.ANY),
        out_shape=jax.ShapeDtypeStruct((slots, W), U32),
        scratch_shapes=[pltpu.SMEM((1, 2 * tm), I32), pltpu.SemaphoreType.DMA(()), pltpu.SemaphoreType.DMA(())],
        input_output_aliases={2: 0},
        compiler_params=_params(("arbitrary",)),
        name="dispatch",
    )(dest.reshape(nt, 1, 2 * tm), h2p, xs0)


def _ffn_kernel(be_ref, nu_ref, x_ref, wg_ref, wu_ref, wd_ref, y_ref):
    del be_ref
    i = pl.program_id(0)

    @pl.when(i < nu_ref[0])
    def _():
        a, b = _unpack_pair(x_ref[...])
        x = jnp.concatenate([a.astype(BF16), b.astype(BF16)], axis=1)
        hg = jnp.dot(x, wg_ref[...], preferred_element_type=F32)
        hu = jnp.dot(x, wu_ref[...], preferred_element_type=F32)
        hid = (hg * jax.nn.sigmoid(hg)) * hu
        y = jnp.dot(hid.astype(BF16), wd_ref[...], preferred_element_type=F32)
        half = D_MODEL // 2
        y_ref[...] = _pack_pair(y[:, :half], y[:, half:])

    @pl.when(i >= nu_ref[0])
    def _():
        y_ref[...] = jnp.zeros(y_ref.shape, U32)


def _ffn(block_expert, n_used, xs, weg, weu, wed, *, bm):
    slots, W = xs.shape
    nb = slots // bm
    grid_spec = pltpu.PrefetchScalarGridSpec(
        num_scalar_prefetch=2,
        grid=(nb,),
        in_specs=[pl.BlockSpec((bm, W), lambda i, be, nu: (i, 0)),
                  pl.BlockSpec((None, D_MODEL, D_EXPERT), lambda i, be, nu: (be[i], 0, 0)),
                  pl.BlockSpec((None, D_MODEL, D_EXPERT), lambda i, be, nu: (be[i], 0, 0)),
                  pl.BlockSpec((None, D_EXPERT, D_MODEL), lambda i, be, nu: (be[i], 0, 0))],
        out_specs=pl.BlockSpec((bm, W), lambda i, be, nu: (i, 0)),
    )
    return pl.pallas_call(
        _ffn_kernel,
        grid_spec=grid_spec,
        out_shape=jax.ShapeDtypeStruct((slots, W), U32),
        compiler_params=_params(("arbitrary",)),
        name="ffn",
    )(block_expert, n_used, xs, weg, weu, wed)


def _combine_kernel(dest_ref, ys_ref, x1_ref, route_ref, gate_ref, fshift_ref, fscale_ref, gf_ref,
                    o_ref, dest_smem, rows_scr, sem, dsem):
    tm = x1_ref.shape[0]
    cp = pltpu.make_async_copy(dest_ref.at[0], dest_smem, dsem)
    cp.start()
    cp.wait()

    def issue(blk, carry):
        for u in range(ROW_UNROLL):
            r = blk * ROW_UNROLL + u
            _row_copy(ys_ref, rows_scr.at[0], dest_smem[0, 2 * r], r, sem).start(priority=0)
            _row_copy(ys_ref, rows_scr.at[1], dest_smem[0, 2 * r + 1], r, sem).start(priority=1)
        return carry

    lax.fori_loop(0, tm // ROW_UNROLL, issue, 0)
    for half in range(2):
        pltpu.make_async_copy(ys_ref.at[pl.ds(0, tm)], rows_scr.at[half], sem).wait()

    route = route_ref[...]
    w1 = route[:, 4:5]
    w2 = route[:, 5:6]
    a1, b1 = _unpack_pair(rows_scr[0])
    a2, b2 = _unpack_pair(rows_scr[1])
    y = jnp.concatenate([a1 * w1 + a2 * w2, b1 * w1 + b2 * w2], axis=1)
    x2 = x1_ref[...] + gate_ref[...] * y
    o_ref[...] = _rmsnorm(x2, gf_ref[...]) * (1.0 + fscale_ref[...]) + fshift_ref[...]


def _combine(dest, ys, x1, route, gate, fshift, fscale, gf, *, tm):
    B, S, D = x1.shape
    nt = S // tm
    W = ys.shape[1]
    per_token = gate.shape[1] != 1
    mod_spec = (pl.BlockSpec((None, tm, D), lambda b, i: (b, i, 0)) if per_token
                else pl.BlockSpec((None, 1, D), lambda b, i: (b, 0, 0)))
    tok = lambda n: pl.BlockSpec((None, tm, n), lambda b, i: (b, i, 0))
    return pl.pallas_call(
        _combine_kernel,
        grid=(B, nt),
        in_specs=[pl.BlockSpec((1, 1, 2 * tm), lambda b, i: (b * nt + i, 0, 0)),
                  pl.BlockSpec(memory_space=pl.ANY),
                  tok(D), tok(LANES), mod_spec, mod_spec, mod_spec,
                  pl.BlockSpec(gf.shape, lambda b, i: (0, 0))],
        out_specs=tok(D),
        out_shape=jax.ShapeDtypeStruct((B, S, D), F32),
        scratch_shapes=[pltpu.SMEM((1, 2 * tm), I32), pltpu.VMEM((2, tm, W), U32),
                        pltpu.SemaphoreType.DMA(()), pltpu.SemaphoreType.DMA(())],
        compiler_params=_params(("arbitrary", "arbitrary")),
        name="combine",
    )(dest.reshape(B * nt, 1, 2 * tm), ys, x1, route, gate, fshift, fscale, gf)


def _rope_tables(pos):
    half = ROPE_DIM // 2
    inv_freq = ROPE_THETA ** (-jnp.arange(0, ROPE_DIM, 2, dtype=F32) / ROPE_DIM)
    ang = pos.astype(F32)[:, None] * inv_freq[None, :]
    cos, sin = jnp.cos(ang), jnp.sin(ang)
    n = pos.shape[0]
    rest = HEAD_DIM - ROPE_DIM
    c64 = jnp.concatenate([cos, cos, jnp.ones((n, rest), F32)], axis=1)
    a64 = jnp.concatenate([-sin, jnp.zeros((n, HEAD_DIM - half), F32)], axis=1)
    b64 = jnp.concatenate([jnp.zeros((n, half), F32), sin, jnp.zeros((n, rest), F32)], axis=1)
    return tuple(jnp.tile(t, (1, LANES // HEAD_DIM)) for t in (c64, a64, b64))


def _moe_plan(route, counts, bm):
    T = route.shape[0]
    e = route[:, 0:2].astype(I32)
    rank = route[:, 2:4].astype(I32)
    cnt = counts[0, :N_EXPERTS].astype(I32)
    padded = (cnt + bm - 1) // bm * bm
    pad_end = jnp.cumsum(padded)
    pad_start = pad_end - padded
    dest = pad_start[e] + rank
    n_blocks = -(-(2 * T) // bm) + N_EXPERTS
    block_expert = jnp.clip(jnp.searchsorted(pad_end, jnp.arange(n_blocks, dtype=I32) * bm, side='right'),
                            0, N_EXPERTS - 1).astype(I32)
    n_used = (pad_end[-1:] // bm).astype(I32)
    return dest.reshape(-1), block_expert, n_used, n_blocks * bm


def _layer(x3, mod, modf, pos_tabs, attend, lam_init, wts, *, tm, tmr, bm, chunked, emit_gv):
    B, S, D = x3.shape
    m = lambda k: mod[:, :, k, :]
    outs = _inproj(x3, m(0), m(1), wts['g1'], wts['win'], pos_tabs, wts['wsp'], wts['bsp'],
                   wts['gln'], wts['bln'], wts['wbb'], tm=tm, chunked=chunked, emit_gv=emit_gv)
    q, kf, vf, kb, vt, sga, gyb = outs[:7]
    gv = outs[7] if emit_gv else None
    o = attend(q, kf, vf, kb, vt)
    x1, h2p, route, counts = _post(o, sga, gyb, x3, m(2), m(3), m(4), wts['g2'], wts['wba'],
                                   wts['wout'], wts['wr'], wts['br'], tm=tm)
    T = B * S
    dest, block_expert, n_used, slots = _moe_plan(route.reshape(T, LANES), counts, bm)
    xs = _dispatch(dest, h2p.reshape(T, D // 2), slots, tm=tmr)
    ys = _ffn(block_expert, n_used, xs, wts['weg'], wts['weu'], wts['wed'], bm=bm)
    y = _combine(dest, ys, x1, route, m(5), modf[:, :, 0, :], modf[:, :, 1, :], wts['gf'], tm=tmr)
    return y, kf, vf, gv


def kernel(x_prompt, x_sample, cache_k, cache_v, page_table, c_prompt, c_sample, w_ada, b_ada, w_ada_final, b_ada_final, g_norm1, g_norm2, g_final, w_in, lambda_q1, lambda_k1, lambda_q2, lambda_k2, g_subln, g_gmlp_ln, b_gmlp_ln, w_spatial, b_spatial, w_branch_a, w_branch_b, w_out, w_router_group, b_router_group, w_router_expert, b_router_expert, w_exp_gate, w_exp_up, w_exp_down):
    depth = w_ada.shape[0]
    assert depth == 1
    B, S, D = x_prompt.shape
    DB, DS, _ = x_sample.shape
    assert DS == 1 and D == D_MODEL
    n_pages = page_table.shape[1]
    n_past = n_pages * PAGE_SIZE
    l = 0
    lam_init = 0.8 - 0.6 * math.exp(-0.3 * l)
    out_scale = 1.0 - lam_init

    c_all = jnp.concatenate([c_prompt, c_sample], axis=0)
    mod_all = _adaln(c_all, w_ada[l], b_ada[l])
    modf_all = _adaln(c_all, w_ada_final, b_ada_final)
    mod_p = mod_all[:B].reshape(B, 1, 6, D)
    mod_s = mod_all[B:].reshape(1, DB, 6, D)
    modf_p = modf_all[:B].reshape(B, 1, 2, D)
    modf_s = modf_all[B:].reshape(1, DB, 2, D)
    lam = _lam(lambda_q1[l], lambda_k1[l], lambda_q2[l], lambda_k2[l], lam_init)[0:1, 0:1]

    row = lambda a: a.reshape(1, -1)
    wr = jnp.zeros((D, LANES), F32)
    wr = wr.at[:, :N_GROUPS].set(w_router_group[l]).at[:, N_GROUPS:N_GROUPS + N_EXPERTS].set(w_router_expert[l])
    br = jnp.zeros((1, LANES), F32)
    br = br.at[0, :N_GROUPS].set(b_router_group[l]).at[0, N_GROUPS:N_GROUPS + N_EXPERTS].set(b_router_expert[l])
    wts = dict(
        g1=row(g_norm1[l]), g2=row(g_norm2[l]), gf=row(g_final), win=w_in[l].astype(BF16),
        gln=row(g_gmlp_ln[l]), bln=row(b_gmlp_ln[l]), wbb=w_branch_b[l].astype(BF16),
        wba=w_branch_a[l].astype(BF16), wout=w_out[l].astype(BF16), wr=wr, br=br,
        weg=w_exp_gate[l].astype(BF16), weu=w_exp_up[l].astype(BF16), wed=w_exp_down[l].astype(BF16))
    gsub = row(g_subln[l])

    wts_s = dict(wts, wsp=row(jnp.repeat(w_spatial[l][:, 0, 0], GMLP_GROUP_DIM)),
                 bsp=row(jnp.repeat(b_spatial[l][:, 0], GMLP_GROUP_DIM)))
    tabs_s = _rope_tables(jnp.full((1,), n_past, dtype=I32))
    ck2 = cache_k[l].reshape(-1, LANES)
    cv2 = cache_v[l].reshape(-1, LANES)
    half_mask = (jnp.arange(LANES)[None, :] // HEAD_DIM == jnp.arange(2)[:, None])

    def attend_s(q, kf, vf, kb, vb):
        del kb, vb
        qh = q.reshape(DB, 1, N_HEADS, LANES)
        q16 = jnp.where(half_mask[None, :, None, :], qh, jnp.zeros_like(qh)).reshape(DB, N_QROWS, LANES)
        kv_of_row = (jnp.arange(N_QROWS) % N_HEADS) // GQA
        kn16 = kf.reshape(DB, N_KV_HEADS, LANES)[:, kv_of_row]
        vn16 = vf.reshape(DB, N_KV_HEADS, V_DIM)[:, kv_of_row]
        pages = 8 if n_pages % 8 == 0 else (4 if n_pages % 4 == 0 else 1)
        o = _attn_sample(page_table, lam, q16, kn16, vn16, gsub, ck2, cv2, pages=pages, out_scale=out_scale)
        return o.reshape(1, DB, N_HEADS * V_DIM)

    y_s, k_s, v_s, gv_s = _layer(x_sample.reshape(1, DB, D), mod_s, modf_s, tabs_s, attend_s, lam_init,
                                 wts_s, tm=DB, tmr=DB, bm=128, chunked=False, emit_gv=True)

    wts_p = dict(wts, wsp=w_spatial[l],
                 bsp=jnp.broadcast_to(b_spatial[l][:, :, None], (GMLP_GROUPS, CHUNK, GMLP_GROUP_DIM)))
    tabs_p = _rope_tables(jnp.arange(S, dtype=I32))
    tq = min(512, S)
    gsub_col = g_subln[l].reshape(V_DIM, 1)
    attend_p = lambda q, kf, vf, kb, vt: _attn_prompt(lam, q, kb, vt, gsub_col, tq=tq, out_scale=out_scale)
    y_p, k_p, v_p, _ = _layer(x_prompt, mod_p, modf_p, tabs_p, attend_p, lam_init, wts_p,
                              tm=min(512, S), tmr=min(256, S), bm=512, chunked=True, emit_gv=False)

    return (y_p,
            y_s.reshape(DB, 1, D),
            k_p.reshape(1, B, S, N_KV_HEADS, 2 * HEAD_DIM),
            v_p.reshape(1, B, S, N_KV_HEADS, V_DIM),
            k_s.reshape(1, DB, 1, N_KV_HEADS, 2 * HEAD_DIM),
            v_s.reshape(1, DB, 1, N_KV_HEADS, V_DIM),
            gv_s.reshape(1, DB, 1, D))
```

```python
import functools
import math

import jax
import jax.numpy as jnp
from jax import lax
from jax.experimental import pallas as pl
from jax.experimental.pallas import tpu as pltpu

F32 = jnp.float32
BF16 = jnp.bfloat16
U32 = jnp.uint32
I32 = jnp.int32

D_MODEL = 1024
N_HEADS = 8
N_KV_HEADS = 4
GQA = N_HEADS // N_KV_HEADS
HEAD_DIM = 64
V_DIM = 128
ROPE_DIM = 16
ROPE_THETA = 500000.0
PAGE_SIZE = 128
GMLP_GROUPS = 8
GMLP_GROUP_DIM = 128
CHUNK = 128
N_GROUPS = 4
EXPERTS_PER_GROUP = 8
N_EXPERTS = 32
D_EXPERT = 512
Q_COLS = 1024
K_COLS = 512
V_COLS = 512
COL_Q, COL_K, COL_V, COL_GU, COL_GV, COL_GA, COL_GB = 0, 1024, 1536, 2048, 3072, 4096, 5120
EPS = 1e-6
NEG_INF = -1e30
LANES = 128
VMEM_LIMIT = 56 * 1024 * 1024

HIGHEST = lax.Precision.HIGHEST
Q_SCALE = HEAD_DIM ** -0.5 * math.log2(math.e)


def _params(sem, flags=None):
    return pltpu.CompilerParams(dimension_semantics=sem, vmem_limit_bytes=VMEM_LIMIT, flags=flags)


def _rmsnorm(x, g):
    return x * lax.rsqrt(jnp.mean(x * x, axis=-1, keepdims=True) + EPS) * g


def _pack_pair(a, b):
    ua = lax.bitcast_convert_type(a.astype(BF16).astype(F32), U32)
    ub = lax.bitcast_convert_type(b.astype(BF16).astype(F32), U32)
    return (ua >> 16) | (ub & jnp.uint32(0xFFFF0000))


def _unpack_pair(w):
    a = lax.bitcast_convert_type(w << 16, F32)
    b = lax.bitcast_convert_type(w & jnp.uint32(0xFFFF0000), F32)
    return a, b


def _adaln_kernel(c_ref, w_ref, b_ref, o_ref):
    c = c_ref[...]
    a = c * jax.nn.sigmoid(c)
    o_ref[...] = jnp.dot(a, w_ref[...], preferred_element_type=F32, precision=HIGHEST) + b_ref[...]


def _adaln(c, w, b, tn=512):
    rows, d = c.shape
    n = w.shape[1]
    return pl.pallas_call(
        _adaln_kernel,
        grid=(n // tn,),
        in_specs=[pl.BlockSpec((rows, d), lambda j: (0, 0)),
                  pl.BlockSpec((d, tn), lambda j: (0, j)),
                  pl.BlockSpec((1, tn), lambda j: (0, j))],
        out_specs=pl.BlockSpec((rows, tn), lambda j: (0, j)),
        out_shape=jax.ShapeDtypeStruct((rows, n), F32),
        compiler_params=_params(("arbitrary",)),
        name="adaln",
    )(c, w, b.reshape(1, n))


def _lam_kernel(q1_ref, k1_ref, q2_ref, k2_ref, o_ref, *, lam_init):
    a = jnp.sum(q1_ref[...] * k1_ref[...], axis=-1, keepdims=True)
    b = jnp.sum(q2_ref[...] * k2_ref[...], axis=-1, keepdims=True)
    lam = jnp.exp(a) - jnp.exp(b) + lam_init
    o_ref[...] = jnp.broadcast_to(lam, o_ref.shape)


def _lam(q1, k1, q2, k2, lam_init):
    args = [a.reshape(1, HEAD_DIM) for a in (q1, k1, q2, k2)]
    return pl.pallas_call(
        functools.partial(_lam_kernel, lam_init=lam_init),
        out_shape=jax.ShapeDtypeStruct((8, LANES), F32),
        name="lam",
    )(*args)


def _inproj_kernel(x_ref, shift_ref, scale_ref, g1_ref, win_ref, rc_ref, ra_ref, rb_ref,
                   wsp_ref, bsp_ref, gln_ref, bln_ref, wbb_ref,
                   q_ref, kf_ref, vf_ref, kb_ref, vt_ref, sga_ref, gyb_ref, *rest,
                   chunked, emit_gv):
    if emit_gv:
        gv_ref, sg_scr = rest
    else:
        (sg_scr,) = rest
    tm = x_ref.shape[0]
    x = x_ref[...]
    h = _rmsnorm(x, g1_ref[...]) * (1.0 + scale_ref[...]) + shift_ref[...]
    hb = h.astype(BF16)

    def proj(c0, n):
        return jnp.dot(hb, win_ref[:, c0:c0 + n], preferred_element_type=F32)

    rc, ra, rb = rc_ref[...], ra_ref[...], rb_ref[...]

    def rope(z):
        return z * rc + pltpu.roll(z, LANES - ROPE_DIM // 2, 1) * ra + pltpu.roll(z, ROPE_DIM // 2, 1) * rb

    zq = proj(COL_Q, Q_COLS)
    for j in range(Q_COLS // LANES):
        z = rope(zq[:, j * LANES:(j + 1) * LANES])
        q_ref[:, j * LANES:(j + 1) * LANES] = (z * Q_SCALE).astype(BF16)
    zk = proj(COL_K, K_COLS)
    for j in range(N_KV_HEADS):
        z = rope(zk[:, j * LANES:(j + 1) * LANES])
        kf_ref[pl.ds(j, tm, stride=N_KV_HEADS), :] = z
        kb_ref[:, j * LANES:(j + 1) * LANES] = z.astype(BF16)
    zv = proj(COL_V, V_COLS)
    for j in range(N_KV_HEADS):
        vf_ref[pl.ds(j, tm, stride=N_KV_HEADS), :] = zv[:, j * V_DIM:(j + 1) * V_DIM]
    vt_ref[...] = zv.T.astype(BF16)

    u = jax.nn.gelu(proj(COL_GU, D_MODEL))
    gv = jax.nn.gelu(proj(COL_GV, D_MODEL))
    xc = gv - jnp.mean(gv, axis=-1, keepdims=True)
    vr = xc * lax.rsqrt(jnp.mean(xc * xc, axis=-1, keepdims=True) + EPS) * gln_ref[...] + bln_ref[...]
    if emit_gv:
        gv_ref[...] = vr
    if chunked:
        vr16 = vr.astype(BF16)
        row = lax.broadcasted_iota(I32, (CHUNK, CHUNK), 0)
        col = lax.broadcasted_iota(I32, (CHUNK, CHUNK), 1)
        for g in range(GMLP_GROUPS):
            w = jnp.where(row >= col, wsp_ref[g], 0.0).astype(BF16)
            gs = slice(g * GMLP_GROUP_DIM, (g + 1) * GMLP_GROUP_DIM)
            for c in range(tm // CHUNK):
                rs = slice(c * CHUNK, (c + 1) * CHUNK)
                sp = jnp.dot(w, vr16[rs, gs], preferred_element_type=F32) + bsp_ref[g]
                sg_scr[rs, gs] = (u[rs, gs] * sp).astype(BF16)
    else:
        sg_scr[...] = (u * (vr * wsp_ref[...] + bsp_ref[...])).astype(BF16)
    yb = jnp.dot(sg_scr[...], wbb_ref[...], preferred_element_type=F32)
    gyb_ref[...] = (jax.nn.sigmoid(proj(COL_GB, D_MODEL)) * yb).astype(BF16)
    sga_ref[...] = jax.nn.sigmoid(proj(COL_GA, D_MODEL)).astype(BF16)


def _inproj(x3, shift, scale, g1, win, rope_tabs, wsp, bsp, gln, bln, wbb, *, tm, chunked, emit_gv):
    B, S, D = x3.shape
    per_token = shift.shape[1] != 1
    nt = S // tm
    mod_spec = (pl.BlockSpec((None, tm, D), lambda b, i: (b, i, 0)) if per_token
                else pl.BlockSpec((None, 1, D), lambda b, i: (b, 0, 0)))
    rope_rows = rope_tabs[0].shape[0]
    rope_spec = (pl.BlockSpec((tm, LANES), lambda b, i: (i, 0)) if rope_rows != 1
                 else pl.BlockSpec((1, LANES), lambda b, i: (0, 0)))
    const2 = lambda a: pl.BlockSpec(a.shape, lambda b, i: (0,) * a.ndim, pipeline_mode=pl.Buffered(1))
    tok = lambda n: pl.BlockSpec((None, tm, n), lambda b, i: (b, i, 0))
    head_rows = pl.BlockSpec((None, tm * N_KV_HEADS, LANES), lambda b, i: (b, i, 0))
    out_shapes = [jax.ShapeDtypeStruct((B, S, Q_COLS), BF16),
                  jax.ShapeDtypeStruct((B, S * N_KV_HEADS, LANES), F32),
                  jax.ShapeDtypeStruct((B, S * N_KV_HEADS, V_DIM), F32),
                  jax.ShapeDtypeStruct((B, S, K_COLS), BF16),
                  jax.ShapeDtypeStruct((B, V_COLS, S), BF16),
                  jax.ShapeDtypeStruct((B, S, D), BF16),
                  jax.ShapeDtypeStruct((B, S, D), BF16)]
    out_specs = [tok(Q_COLS), head_rows, head_rows, tok(K_COLS),
                 pl.BlockSpec((None, V_COLS, tm), lambda b, i: (b, 0, i)), tok(D), tok(D)]
    if emit_gv:
        out_shapes.append(jax.ShapeDtypeStruct((B, S, D), F32))
        out_specs.append(tok(D))
    return pl.pallas_call(
        functools.partial(_inproj_kernel, chunked=chunked, emit_gv=emit_gv),
        grid=(B, nt),
        in_specs=[tok(D), mod_spec, mod_spec, const2(g1), const2(win),
                  rope_spec, rope_spec, rope_spec, const2(wsp), const2(bsp),
                  const2(gln), const2(bln), const2(wbb)],
        out_specs=out_specs,
        out_shape=out_shapes,
        scratch_shapes=[pltpu.VMEM((tm, D), BF16)],
        compiler_params=_params(("arbitrary", "arbitrary")),
        name="inproj",
    )(x3, shift, scale, g1, win, *rope_tabs, wsp, bsp, gln, bln, wbb)


ATTN_COL_SPLIT = 2


def _attn_prompt_kernel(lam_ref, q_ref, k_ref, vt_ref, gsub_ref, o_ref, m_scr, l_scr, acc_scr,
                        *, tq, out_scale):
    i = pl.program_id(2)
    q = q_ref[...]
    lane = lax.broadcasted_iota(I32, (tq, LANES), 1)
    lo = lane < HEAD_DIM
    zero = jnp.zeros((tq, LANES), BF16)
    heads = [q[:, g * LANES:(g + 1) * LANES] for g in range(GQA)]
    pw = tq // ATTN_COL_SPLIT
    masked_heads = [jnp.where(lo, h, zero) for h in heads] + [jnp.where(lo, zero, h) for h in heads]
    qp = [h[c * pw:(c + 1) * pw] for h in masked_heads for c in range(ATTN_COL_SPLIT)]
    n_pieces = len(qp)
    m_scr[...] = jnp.full(m_scr.shape, NEG_INF, F32)
    l_scr[...] = jnp.zeros(l_scr.shape, F32)
    acc_scr[...] = jnp.zeros(acc_scr.shape, F32)

    def step(j, masked):
        start = pl.multiple_of(j * tq, tq)
        k = k_ref[pl.ds(start, tq), :]
        vt = vt_ref[:, pl.ds(start, tq)]
        scores = [lax.dot_general(k, qp[p], (((1,), (1,)), ((), ())), preferred_element_type=F32)
                  for p in range(n_pieces)]
        for p in range(n_pieces):
            s = scores[p]
            if masked:
                kpos = lax.broadcasted_iota(I32, (tq, pw), 0)
                qpos = lax.broadcasted_iota(I32, (tq, pw), 1) + (p % ATTN_COL_SPLIT) * pw
                s = jnp.where(qpos >= kpos, s, NEG_INF)
            m_prev = m_scr[p]
            m_new = jnp.maximum(m_prev, jnp.max(s, axis=0, keepdims=True))
            alpha = jnp.exp2(m_prev - m_new)
            e = jnp.exp2(s - m_new)
            l_scr[p] = alpha * l_scr[p] + jnp.sum(e, axis=0, keepdims=True)
            acc_scr[p] = alpha * acc_scr[p] + jnp.dot(vt, e.astype(BF16), preferred_element_type=F32)
            m_scr[p] = m_new

    def body(j, carry):
        step(j, False)
        return carry

    lax.fori_loop(0, i, body, 0)
    step(i, True)

    lam = lam_ref[0, 0]
    for g in range(GQA):
        for c in range(ATTN_COL_SPLIT):
            p1 = g * ATTN_COL_SPLIT + c
            p2 = (GQA + g) * ATTN_COL_SPLIT + c
            ot = acc_scr[p1] / l_scr[p1] - lam * (acc_scr[p2] / l_scr[p2])
            ot = ot * lax.rsqrt(jnp.mean(ot * ot, axis=0, keepdims=True) + EPS) * gsub_ref[...] * out_scale
            o_ref[c * pw:(c + 1) * pw, g * V_DIM:(g + 1) * V_DIM] = ot.T.astype(o_ref.dtype)


def _attn_prompt(lam, q, kb, vt, gsub_col, *, tq, out_scale):
    B, S, _ = q.shape
    n_pieces = 2 * GQA * ATTN_COL_SPLIT
    pw = tq // ATTN_COL_SPLIT
    return pl.pallas_call(
        functools.partial(_attn_prompt_kernel, tq=tq, out_scale=out_scale),
        grid=(B, N_KV_HEADS, S // tq),
        in_specs=[pl.BlockSpec(memory_space=pltpu.SMEM),
                  pl.BlockSpec((None, tq, GQA * LANES), lambda b, n, i: (b, i, n)),
                  pl.BlockSpec((None, S, LANES), lambda b, n, i: (b, 0, n)),
                  pl.BlockSpec((None, V_DIM, S), lambda b, n, i: (b, n, 0)),
                  pl.BlockSpec((V_DIM, 1), lambda b, n, i: (0, 0))],
        out_specs=pl.BlockSpec((None, tq, GQA * V_DIM), lambda b, n, i: (b, i, n)),
        out_shape=jax.ShapeDtypeStruct((B, S, N_HEADS * V_DIM), BF16),
        scratch_shapes=[pltpu.VMEM((n_pieces, 1, pw), F32), pltpu.VMEM((n_pieces, 1, pw), F32),
                        pltpu.VMEM((n_pieces, V_DIM, pw), F32)],
        compiler_params=_params(("arbitrary", "arbitrary", "arbitrary")),
        name="attn_prompt",
    )(lam, q, kb, vt, gsub_col)


N_QROWS = 2 * N_HEADS
PAGE_ROWS = PAGE_SIZE * N_KV_HEADS


def _attn_sample_kernel(pt_ref, lam_ref, q_ref, kn_ref, vn_ref, gsub_ref, ck_ref, cv_ref, o_ref,
                        kbuf, vbuf, sems, m_scr, l_scr, acc_scr, *, pages, out_scale):
    b = pl.program_id(0)
    c = pl.program_id(1)
    n_c = pl.num_programs(1)
    step = b * n_c + c
    n_steps = pl.num_programs(0) * n_c
    slot = lax.rem(step, 2)

    def page_copies(bb, cc, sl):
        out = []
        for p in range(pages):
            page = pt_ref[bb, cc * pages + p]
            out.append(pltpu.make_async_copy(ck_ref.at[page], kbuf.at[sl, p], sems.at[sl]))
            out.append(pltpu.make_async_copy(cv_ref.at[page], vbuf.at[sl, p], sems.at[sl]))
        return out

    @pl.when(step == 0)
    def _():
        for cp in page_copies(0, 0, 0):
            cp.start()

    @pl.when(step + 1 < n_steps)
    def _():
        last_c = c + 1 == n_c
        for cp in page_copies(jnp.where(last_c, b + 1, b), jnp.where(last_c, 0, c + 1), 1 - slot):
            cp.start()

    for cp in page_copies(b, c, slot):
        cp.wait()

    @pl.when(c == 0)
    def _():
        m_scr[...] = jnp.full(m_scr.shape, NEG_INF, F32)
        l_scr[...] = jnp.zeros(l_scr.shape, F32)
        acc_scr[...] = jnp.zeros(acc_scr.shape, F32)

    q = q_ref[...]
    r = lax.broadcasted_iota(I32, (N_QROWS, PAGE_ROWS), 0)
    col = lax.broadcasted_iota(I32, (N_QROWS, PAGE_ROWS), 1)
    valid = (col & (N_KV_HEADS - 1)) == ((r & (N_HEADS - 1)) >> 1)
    k_pages = [kbuf[slot, p] for p in range(pages)]
    v_pages = [vbuf[slot, p] for p in range(pages)]
    scores = []
    for p in range(pages):
        s = lax.dot_general(q, k_pages[p].astype(BF16), (((1,), (1,)), ((), ())),
                            preferred_element_type=F32)
        scores.append(jnp.where(valid, s, NEG_INF))
    m_prev = m_scr[...]
    m_new = m_prev
    for s in scores:
        m_new = jnp.maximum(m_new, jnp.max(s, axis=-1, keepdims=True))
    alpha = jnp.exp2(m_prev - m_new)
    l_new = alpha * l_scr[...]
    acc = alpha * acc_scr[...]
    for p in range(pages):
        e = jnp.where(valid, jnp.exp2(scores[p] - m_new), 0.0)
        l_new = l_new + jnp.sum(e, axis=-1, keepdims=True)
        acc = acc + jnp.dot(e.astype(BF16), v_pages[p].astype(BF16), preferred_element_type=F32)
    m_scr[...] = m_new
    l_scr[...] = l_new
    acc_scr[...] = acc

    @pl.when(c == pl.num_programs(1) - 1)
    def _():
        s_self = jnp.sum(q.astype(F32) * kn_ref[...], axis=-1, keepdims=True)
        m_fin = jnp.maximum(m_new, s_self)
        a = jnp.exp2(m_new - m_fin)
        e_self = jnp.exp2(s_self - m_fin)
        o = (a * acc + e_self * vn_ref[...]) / (a * l_new + e_self)
        lam = lam_ref[0, 0]
        od = o[:N_HEADS] - lam * o[N_HEADS:]
        o_ref[...] = (_rmsnorm(od, gsub_ref[...]) * out_scale).astype(o_ref.dtype)


def _attn_sample(page_table, lam, q16, kn16, vn16, gsub, ck3, cv3, *, pages, out_scale):
    DB, n_pages = page_table.shape
    steps = n_pages // pages
    row3 = lambda n: pl.BlockSpec((None, n, LANES), lambda b, c, pt: (b, 0, 0))
    grid_spec = pltpu.PrefetchScalarGridSpec(
        num_scalar_prefetch=1,
        grid=(DB, steps),
        in_specs=[pl.BlockSpec(memory_space=pltpu.SMEM), row3(N_QROWS), row3(N_QROWS), row3(N_QROWS),
                  pl.BlockSpec((1, V_DIM), lambda b, c, pt: (0, 0)),
                  pl.BlockSpec(memory_space=pl.ANY), pl.BlockSpec(memory_space=pl.ANY)],
        out_specs=row3(N_HEADS),
        scratch_shapes=[pltpu.VMEM((2, pages, PAGE_ROWS, LANES), F32),
                        pltpu.VMEM((2, pages, PAGE_ROWS, LANES), F32),
                        pltpu.SemaphoreType.DMA((2,)),
                        pltpu.VMEM((N_QROWS, 1), F32), pltpu.VMEM((N_QROWS, 1), F32),
                        pltpu.VMEM((N_QROWS, V_DIM), F32)],
    )
    return pl.pallas_call(
        functools.partial(_attn_sample_kernel, pages=pages, out_scale=out_scale),
        grid_spec=grid_spec,
        out_shape=jax.ShapeDtypeStruct((DB, N_HEADS, V_DIM), BF16),
        compiler_params=_params(("arbitrary", "arbitrary")),
        name="attn_sample",
    )(page_table, lam, q16, kn16, vn16, gsub, ck3, cv3)


def _post_kernel(o_ref, sga_ref, gyb_ref, x_ref, gate_ref, shift_ref, scale_ref, g2_ref,
                 wba_ref, wout_ref, wr_ref, br_ref,
                 x1_ref, h2p_ref, route_ref, counts_ref, base_scr):
    tm = x_ref.shape[0]
    first = (pl.program_id(0) == 0) & (pl.program_id(1) == 0)

    @pl.when(first)
    def _():
        base_scr[...] = jnp.zeros(base_scr.shape, F32)

    ya = jnp.dot(o_ref[...], wba_ref[...], preferred_element_type=F32)
    merged = sga_ref[...].astype(F32) * ya + gyb_ref[...].astype(F32)
    upd = jnp.dot(merged.astype(BF16), wout_ref[...], preferred_element_type=F32)
    x1 = x_ref[...] + gate_ref[...] * upd
    x1_ref[...] = x1
    h2 = _rmsnorm(x1, g2_ref[...]) * (1.0 + scale_ref[...]) + shift_ref[...]
    half = D_MODEL // 2
    h2p_ref[...] = _pack_pair(h2[:, :half], h2[:, half:])

    h_hi = h2.astype(BF16)
    h_lo = (h2 - h_hi.astype(F32)).astype(BF16)
    logits = (jnp.dot(h_hi, wr_ref[0], preferred_element_type=F32)
              + (jnp.dot(h_hi, wr_ref[1], preferred_element_type=F32)
                 + jnp.dot(h_lo, wr_ref[0], preferred_element_type=F32))) + br_ref[...]
    lane = lax.broadcasted_iota(I32, (tm, LANES), 1)
    lanef = lane.astype(F32)
    big = float(LANES)
    is_g = lane < N_GROUPS
    lg = jnp.where(is_g, logits, -jnp.inf)
    gmax = jnp.max(lg, axis=-1, keepdims=True)
    g_idx = jnp.min(jnp.where(lg == gmax, lanef, big), axis=-1, keepdims=True)
    p_grp = 1.0 / jnp.sum(jnp.where(is_g, jnp.exp(logits - gmax), 0.0), axis=-1, keepdims=True)
    e_lo = N_GROUPS + EXPERTS_PER_GROUP * g_idx
    in_grp = (lanef >= e_lo) & (lanef < e_lo + EXPERTS_PER_GROUP)
    emax = jnp.max(jnp.where(in_grp, logits, -jnp.inf), axis=-1, keepdims=True)
    ee = jnp.where(in_grp, jnp.exp(logits - emax), 0.0)
    pe = ee / jnp.sum(ee, axis=-1, keepdims=True)
    pe_m = jnp.where(in_grp, pe, -1.0)
    v1 = jnp.max(pe_m, axis=-1, keepdims=True)
    i1 = jnp.min(jnp.where(pe_m == v1, lanef, big), axis=-1, keepdims=True)
    pe_m2 = jnp.where(lanef == i1, -1.0, pe_m)
    v2 = jnp.max(pe_m2, axis=-1, keepdims=True)
    i2 = jnp.min(jnp.where(pe_m2 == v2, lanef, big), axis=-1, keepdims=True)
    tsum = v1 + v2
    w1 = p_grp * (v1 / tsum)
    w2 = p_grp * (v2 / tsum)
    e1 = i1 - N_GROUPS
    e2 = i2 - N_GROUPS

    oh1 = (lanef == e1).astype(F32)
    oh2 = (lanef == e2).astype(F32)
    ohs = oh1 + oh2
    rr = lax.broadcasted_iota(I32, (tm, tm), 0)
    cc = lax.broadcasted_iota(I32, (tm, tm), 1)
    tri = jnp.where(rr > cc, 1.0, 0.0).astype(BF16)
    before = jnp.dot(tri, ohs.astype(BF16), preferred_element_type=F32) + base_scr[...]
    r1 = jnp.sum(before * oh1, axis=-1, keepdims=True)
    r2 = jnp.sum(before * oh2, axis=-1, keepdims=True)
    new_base = base_scr[...] + jnp.sum(ohs, axis=0, keepdims=True)
    base_scr[...] = new_base
    counts_ref[...] = jnp.broadcast_to(new_base, counts_ref.shape)

    route = jnp.zeros((tm, LANES), F32)
    for idx, val in enumerate((e1, e2, r1, r2, w1, w2)):
        route = jnp.where(lane == idx, val, route)
    route_ref[...] = route


def _post(o, sga, gyb, x3, gate, shift, scale, g2, wba, wout, wr, br, *, tm):
    B, S, D = x3.shape
    per_token = gate.shape[1] != 1
    mod_spec = (pl.BlockSpec((None, tm, D), lambda b, i: (b, i, 0)) if per_token
                else pl.BlockSpec((None, 1, D), lambda b, i: (b, 0, 0)))
    const2 = lambda a: pl.BlockSpec(a.shape, lambda b, i: (0,) * a.ndim, pipeline_mode=pl.Buffered(1))
    tok = lambda n: pl.BlockSpec((None, tm, n), lambda b, i: (b, i, 0))
    return pl.pallas_call(
        _post_kernel,
        grid=(B, S // tm),
        in_specs=[tok(D), tok(D), tok(D), tok(D), mod_spec, mod_spec, mod_spec, const2(g2),
                  const2(wba), const2(wout), const2(wr), const2(br)],
        out_specs=[tok(D), tok(D // 2), tok(LANES), pl.BlockSpec((8, LANES), lambda b, i: (0, 0))],
        out_shape=[jax.ShapeDtypeStruct((B, S, D), F32),
                   jax.ShapeDtypeStruct((B, S, D // 2), U32),
                   jax.ShapeDtypeStruct((B, S, LANES), F32),
                   jax.ShapeDtypeStruct((8, LANES), F32)],
        scratch_shapes=[pltpu.VMEM((1, LANES), F32)],
        compiler_params=_params(("arbitrary", "arbitrary")),
        name="post",
    )(o, sga, gyb, x3, gate, shift, scale, g2, wba, wout, wr, br)


ROW_UNROLL = 8


def _row_copy(src, dst, src_row, dst_row, sem):
    return pltpu.make_async_copy(src.at[pl.ds(src_row, 1)], dst.at[pl.ds(dst_row, 1)], sem)


def _dispatch_kernel(dest_ref, h_ref, xs_in_ref, xs_ref, dest_smem, sem, dsem):
    del xs_in_ref
    tm = h_ref.shape[0]
    cp = pltpu.make_async_copy(dest_ref.at[0], dest_smem, dsem)
    cp.start()
    cp.wait()

    def issue(blk, carry):
        for u in range(ROW_UNROLL):
            r = blk * ROW_UNROLL + u
            _row_copy(h_ref, xs_ref, r, dest_smem[0, 2 * r], sem).start(priority=0)
            _row_copy(h_ref, xs_ref, r, dest_smem[0, 2 * r + 1], sem).start(priority=1)
        return carry

    lax.fori_loop(0, tm // ROW_UNROLL, issue, 0)
    for _ in range(2):
        pltpu.make_async_copy(h_ref, xs_ref.at[pl.ds(0, tm)], sem).wait()


def _dispatch(dest, h2p, slots, *, tm):
    T, W = h2p.shape
    nt = T // tm
    xs0 = jnp.zeros((slots, W), U32)
    return pl.pallas_call(
        _dispatch_kernel,
        grid=(nt,),
        in_specs=[pl.BlockSpec((1, 1, 2 * tm), lambda i: (i, 0, 0)),
                  pl.BlockSpec((tm, W), lambda i: (i, 0)),
                  pl.BlockSpec(memory_space=pl.ANY)],
        out_specs=pl.BlockSpec(memory_space=pl.ANY),
        out_shape=jax.ShapeDtypeStruct((slots, W), U32),
        scratch_shapes=[pltpu.SMEM((1, 2 * tm), I32), pltpu.SemaphoreType.DMA(()), pltpu.SemaphoreType.DMA(())],
        input_output_aliases={2: 0},
        compiler_params=_params(("arbitrary",)),
        name="dispatch",
    )(dest.reshape(nt, 1, 2 * tm), h2p, xs0)


def _ffn_kernel(be_ref, nu_ref, x_ref, wg_ref, wu_ref, wd_ref, y_ref):
    del be_ref
    i = pl.program_id(0)

    @pl.when(i < nu_ref[0])
    def _():
        a, b = _unpack_pair(x_ref[...])
        x = jnp.concatenate([a.astype(BF16), b.astype(BF16)], axis=1)
        hg = jnp.dot(x, wg_ref[...], preferred_element_type=F32)
        hu = jnp.dot(x, wu_ref[...], preferred_element_type=F32)
        hid = (hg * jax.nn.sigmoid(hg)) * hu
        y = jnp.dot(hid.astype(BF16), wd_ref[...], preferred_element_type=F32)
        half = D_MODEL // 2
        y_ref[...] = _pack_pair(y[:, :half], y[:, half:])

    @pl.when(i >= nu_ref[0])
    def _():
        y_ref[...] = jnp.zeros(y_ref.shape, U32)


def _ffn(block_expert, n_used, xs, weg, weu, wed, *, bm):
    slots, W = xs.shape
    nb = slots // bm
    grid_spec = pltpu.PrefetchScalarGridSpec(
        num_scalar_prefetch=2,
        grid=(nb,),
        in_specs=[pl.BlockSpec((bm, W), lambda i, be, nu: (i, 0)),
                  pl.BlockSpec((None, D_MODEL, D_EXPERT), lambda i, be, nu: (be[i], 0, 0)),
                  pl.BlockSpec((None, D_MODEL, D_EXPERT), lambda i, be, nu: (be[i], 0, 0)),
                  pl.BlockSpec((None, D_EXPERT, D_MODEL), lambda i, be, nu: (be[i], 0, 0))],
        out_specs=pl.BlockSpec((bm, W), lambda i, be, nu: (i, 0)),
    )
    return pl.pallas_call(
        _ffn_kernel,
        grid_spec=grid_spec,
        out_shape=jax.ShapeDtypeStruct((slots, W), U32),
        compiler_params=_params(("arbitrary",)),
        name="ffn",
    )(block_expert, n_used, xs, weg, weu, wed)


def _combine_kernel(dest_ref, dest_next_ref, ys_ref, x1_ref, route_ref, gate_ref, fshift_ref, fscale_ref,
                    gf_ref, o_ref, dest_smem, rows_scr, sems, dsem):
    tm = x1_ref.shape[0]
    step = pl.program_id(0) * pl.num_programs(1) + pl.program_id(1)
    n_steps = pl.num_programs(0) * pl.num_programs(1)
    slot = lax.rem(step, 2)

    def gather(dref, sl):
        cp = pltpu.make_async_copy(dref.at[0], dest_smem, dsem)
        cp.start()
        cp.wait()

        def issue(blk, carry):
            for u in range(ROW_UNROLL):
                r = blk * ROW_UNROLL + u
                _row_copy(ys_ref, rows_scr.at[sl, 0], dest_smem[0, 2 * r], r, sems.at[sl]).start(priority=0)
                _row_copy(ys_ref, rows_scr.at[sl, 1], dest_smem[0, 2 * r + 1], r, sems.at[sl]).start(priority=1)
            return carry

        lax.fori_loop(0, tm // ROW_UNROLL, issue, 0)

    @pl.when(step == 0)
    def _():
        gather(dest_ref, 0)

    @pl.when(step + 1 < n_steps)
    def _():
        gather(dest_next_ref, 1 - slot)

    for half in range(2):
        pltpu.make_async_copy(ys_ref.at[pl.ds(0, tm)], rows_scr.at[slot, half], sems.at[slot]).wait()

    route = route_ref[...]
    w1 = route[:, 4:5]
    w2 = route[:, 5:6]
    a1, b1 = _unpack_pair(rows_scr[slot, 0])
    a2, b2 = _unpack_pair(rows_scr[slot, 1])
    y = jnp.concatenate([a1 * w1 + a2 * w2, b1 * w1 + b2 * w2], axis=1)
    x2 = x1_ref[...] + gate_ref[...] * y
    o_ref[...] = _rmsnorm(x2, gf_ref[...]) * (1.0 + fscale_ref[...]) + fshift_ref[...]


def _combine(dest, ys, x1, route, gate, fshift, fscale, gf, *, tm):
    B, S, D = x1.shape
    nt = S // tm
    W = ys.shape[1]
    per_token = gate.shape[1] != 1
    mod_spec = (pl.BlockSpec((None, tm, D), lambda b, i: (b, i, 0)) if per_token
                else pl.BlockSpec((None, 1, D), lambda b, i: (b, 0, 0)))
    tok = lambda n: pl.BlockSpec((None, tm, n), lambda b, i: (b, i, 0))
    dest3 = dest.reshape(B * nt, 1, 2 * tm)
    return pl.pallas_call(
        _combine_kernel,
        grid=(B, nt),
        in_specs=[pl.BlockSpec((1, 1, 2 * tm), lambda b, i: (b * nt + i, 0, 0)),
                  pl.BlockSpec((1, 1, 2 * tm), lambda b, i: (jnp.minimum(b * nt + i + 1, B * nt - 1), 0, 0)),
                  pl.BlockSpec(memory_space=pl.ANY),
                  tok(D), tok(LANES), mod_spec, mod_spec, mod_spec,
                  pl.BlockSpec(gf.shape, lambda b, i: (0, 0))],
        out_specs=tok(D),
        out_shape=jax.ShapeDtypeStruct((B, S, D), F32),
        scratch_shapes=[pltpu.SMEM((1, 2 * tm), I32), pltpu.VMEM((2, 2, tm, W), U32),
                        pltpu.SemaphoreType.DMA((2,)), pltpu.SemaphoreType.DMA(())],
        compiler_params=_params(("arbitrary", "arbitrary")),
        name="combine",
    )(dest3, dest3, ys, x1, route, gate, fshift, fscale, gf)


def _rope_tables(pos):
    half = ROPE_DIM // 2
    inv_freq = ROPE_THETA ** (-jnp.arange(0, ROPE_DIM, 2, dtype=F32) / ROPE_DIM)
    ang = pos.astype(F32)[:, None] * inv_freq[None, :]
    cos, sin = jnp.cos(ang), jnp.sin(ang)
    n = pos.shape[0]
    rest = HEAD_DIM - ROPE_DIM
    c64 = jnp.concatenate([cos, cos, jnp.ones((n, rest), F32)], axis=1)
    a64 = jnp.concatenate([-sin, jnp.zeros((n, HEAD_DIM - half), F32)], axis=1)
    b64 = jnp.concatenate([jnp.zeros((n, half), F32), sin, jnp.zeros((n, rest), F32)], axis=1)
    return tuple(jnp.tile(t, (1, LANES // HEAD_DIM)) for t in (c64, a64, b64))


def _moe_plan(route, counts, bm):
    T = route.shape[0]
    e = route[:, 0:2].astype(I32)
    rank = route[:, 2:4].astype(I32)
    cnt = counts[0, :N_EXPERTS].astype(I32)
    padded = (cnt + bm - 1) // bm * bm
    pad_end = jnp.cumsum(padded)
    pad_start = pad_end - padded
    ids = jnp.arange(N_EXPERTS, dtype=I32)
    dest = rank + jnp.sum(jnp.where(e[..., None] == ids, pad_start, 0), axis=-1)
    n_blocks = -(-(2 * T) // bm) + N_EXPERTS
    block_lo = jnp.arange(n_blocks, dtype=I32) * bm
    block_expert = jnp.minimum(jnp.sum((pad_end[None, :] <= block_lo[:, None]).astype(I32), axis=1),
                               N_EXPERTS - 1)
    n_used = (pad_end[-1:] // bm).astype(I32)
    return dest.reshape(-1), block_expert, n_used, n_blocks * bm


def _layer(x3, mod, modf, pos_tabs, attend, lam_init, wts, *, tm, tmr, bm, chunked, emit_gv):
    B, S, D = x3.shape
    m = lambda k: mod[:, :, k, :]
    outs = _inproj(x3, m(0), m(1), wts['g1'], wts['win'], pos_tabs, wts['wsp'], wts['bsp'],
                   wts['gln'], wts['bln'], wts['wbb'], tm=tm, chunked=chunked, emit_gv=emit_gv)
    q, kf, vf, kb, vt, sga, gyb = outs[:7]
    gv = outs[7] if emit_gv else None
    o = attend(q, kf, vf, kb, vt)
    x1, h2p, route, counts = _post(o, sga, gyb, x3, m(2), m(3), m(4), wts['g2'], wts['wba'],
                                   wts['wout'], wts['wr'], wts['br'], tm=tm)
    T = B * S
    dest, block_expert, n_used, slots = _moe_plan(route.reshape(T, LANES), counts, bm)
    xs = _dispatch(dest, h2p.reshape(T, D // 2), slots, tm=tmr)
    ys = _ffn(block_expert, n_used, xs, wts['weg'], wts['weu'], wts['wed'], bm=bm)
    y = _combine(dest, ys, x1, route, m(5), modf[:, :, 0, :], modf[:, :, 1, :], wts['gf'], tm=tmr)
    return y, kf, vf, gv


def kernel(x_prompt, x_sample, cache_k, cache_v, page_table, c_prompt, c_sample, w_ada, b_ada, w_ada_final, b_ada_final, g_norm1, g_norm2, g_final, w_in, lambda_q1, lambda_k1, lambda_q2, lambda_k2, g_subln, g_gmlp_ln, b_gmlp_ln, w_spatial, b_spatial, w_branch_a, w_branch_b, w_out, w_router_group, b_router_group, w_router_expert, b_router_expert, w_exp_gate, w_exp_up, w_exp_down):
    depth = w_ada.shape[0]
    assert depth == 1
    B, S, D = x_prompt.shape
    DB, DS, _ = x_sample.shape
    assert DS == 1 and D == D_MODEL
    n_pages = page_table.shape[1]
    n_past = n_pages * PAGE_SIZE
    l = 0
    lam_init = 0.8 - 0.6 * math.exp(-0.3 * l)
    out_scale = 1.0 - lam_init

    c_all = jnp.concatenate([c_prompt, c_sample], axis=0)
    mod_all = _adaln(c_all, w_ada[l], b_ada[l])
    modf_all = _adaln(c_all, w_ada_final, b_ada_final)
    mod_p = mod_all[:B].reshape(B, 1, 6, D)
    mod_s = mod_all[B:].reshape(1, DB, 6, D)
    modf_p = modf_all[:B].reshape(B, 1, 2, D)
    modf_s = modf_all[B:].reshape(1, DB, 2, D)
    lam = _lam(lambda_q1[l], lambda_k1[l], lambda_q2[l], lambda_k2[l], lam_init)[0:1, 0:1]

    row = lambda a: a.reshape(1, -1)
    wr = jnp.zeros((D, LANES), F32)
    wr = wr.at[:, :N_GROUPS].set(w_router_group[l]).at[:, N_GROUPS:N_GROUPS + N_EXPERTS].set(w_router_expert[l])
    wr_hi = wr.astype(BF16)
    wr_lo = (wr - wr_hi.astype(F32)).astype(BF16)
    br = jnp.zeros((1, LANES), F32)
    br = br.at[0, :N_GROUPS].set(b_router_group[l]).at[0, N_GROUPS:N_GROUPS + N_EXPERTS].set(b_router_expert[l])
    wts = dict(
        g1=row(g_norm1[l]), g2=row(g_norm2[l]), gf=row(g_final), win=w_in[l].astype(BF16),
        gln=row(g_gmlp_ln[l]), bln=row(b_gmlp_ln[l]), wbb=w_branch_b[l].astype(BF16),
        wba=w_branch_a[l].astype(BF16), wout=w_out[l].astype(BF16), wr=jnp.stack([wr_hi, wr_lo]), br=br,
        weg=w_exp_gate[l].astype(BF16), weu=w_exp_up[l].astype(BF16), wed=w_exp_down[l].astype(BF16))
    gsub = row(g_subln[l])

    wts_s = dict(wts, wsp=row(jnp.repeat(w_spatial[l][:, 0, 0], GMLP_GROUP_DIM)),
                 bsp=row(jnp.repeat(b_spatial[l][:, 0], GMLP_GROUP_DIM)))
    tabs_s = _rope_tables(jnp.full((1,), n_past, dtype=I32))
    ck3 = cache_k[l].reshape(-1, PAGE_ROWS, LANES)
    cv3 = cache_v[l].reshape(-1, PAGE_ROWS, LANES)
    half_mask = (jnp.arange(LANES)[None, :] // HEAD_DIM == jnp.arange(2)[:, None])

    def attend_s(q, kf, vf, kb, vb):
        del kb, vb
        qh = q.reshape(DB, 1, N_HEADS, LANES)
        q16 = jnp.where(half_mask[None, :, None, :], qh, jnp.zeros_like(qh)).reshape(DB, N_QROWS, LANES)
        kv_of_row = (jnp.arange(N_QROWS) % N_HEADS) // GQA
        kn16 = kf.reshape(DB, N_KV_HEADS, LANES)[:, kv_of_row]
        vn16 = vf.reshape(DB, N_KV_HEADS, V_DIM)[:, kv_of_row]
        pages = 8 if n_pages % 8 == 0 else (4 if n_pages % 4 == 0 else 1)
        o = _attn_sample(page_table, lam, q16, kn16, vn16, gsub, ck3, cv3, pages=pages, out_scale=out_scale)
        return o.reshape(1, DB, N_HEADS * V_DIM)

    y_s, k_s, v_s, gv_s = _layer(x_sample.reshape(1, DB, D), mod_s, modf_s, tabs_s, attend_s, lam_init,
                                 wts_s, tm=DB, tmr=DB, bm=128, chunked=False, emit_gv=True)

    wts_p = dict(wts, wsp=w_spatial[l],
                 bsp=jnp.broadcast_to(b_spatial[l][:, :, None], (GMLP_GROUPS, CHUNK, GMLP_GROUP_DIM)))
    tabs_p = _rope_tables(jnp.arange(S, dtype=I32))
    tq = min(512, S)
    gsub_col = g_subln[l].reshape(V_DIM, 1)
    attend_p = lambda q, kf, vf, kb, vt: _attn_prompt(lam, q, kb, vt, gsub_col, tq=tq, out_scale=out_scale)
    y_p, k_p, v_p, _ = _layer(x_prompt, mod_p, modf_p, tabs_p, attend_p, lam_init, wts_p,
                              tm=min(512, S), tmr=min(256, S), bm=512, chunked=True, emit_gv=False)

    return (y_p,
            y_s.reshape(DB, 1, D),
            k_p.reshape(1, B, S, N_KV_HEADS, 2 * HEAD_DIM),
            v_p.reshape(1, B, S, N_KV_HEADS, V_DIM),
            k_s.reshape(1, DB, 1, N_KV_HEADS, 2 * HEAD_DIM),
            v_s.reshape(1, DB, 1, N_KV_HEADS, V_DIM),
            gv_s.reshape(1, DB, 1, D))
```

```python
import functools
import math

import jax
import jax.numpy as jnp
from jax import lax
from jax.experimental import pallas as pl
from jax.experimental.pallas import tpu as pltpu
from jax.experimental.pallas import tpu_sc as plsc

F32 = jnp.float32
BF16 = jnp.bfloat16
U32 = jnp.uint32
I32 = jnp.int32

D_MODEL = 1024
N_HEADS = 8
N_KV_HEADS = 4
GQA = N_HEADS // N_KV_HEADS
HEAD_DIM = 64
V_DIM = 128
ROPE_DIM = 16
ROPE_THETA = 500000.0
PAGE_SIZE = 128
GMLP_GROUPS = 8
GMLP_GROUP_DIM = 128
CHUNK = 128
N_GROUPS = 4
EXPERTS_PER_GROUP = 8
N_EXPERTS = 32
D_EXPERT = 512
Q_COLS = 1024
K_COLS = 512
V_COLS = 512
COL_Q, COL_K, COL_V, COL_GU, COL_GV, COL_GA, COL_GB = 0, 1024, 1536, 2048, 3072, 4096, 5120
EPS = 1e-6
NEG_INF = -1e30
LANES = 128
VMEM_LIMIT = 56 * 1024 * 1024

HIGHEST = lax.Precision.HIGHEST
Q_SCALE = HEAD_DIM ** -0.5 * math.log2(math.e)


def _params(sem, flags=None):
    return pltpu.CompilerParams(dimension_semantics=sem, vmem_limit_bytes=VMEM_LIMIT, flags=flags)


def _rmsnorm(x, g):
    return x * lax.rsqrt(jnp.mean(x * x, axis=-1, keepdims=True) + EPS) * g


def _pack_pair(a, b):
    ua = lax.bitcast_convert_type(a.astype(BF16).astype(F32), U32)
    ub = lax.bitcast_convert_type(b.astype(BF16).astype(F32), U32)
    return (ua >> 16) | (ub & jnp.uint32(0xFFFF0000))


def _unpack_pair(w):
    a = lax.bitcast_convert_type(w << 16, F32)
    b = lax.bitcast_convert_type(w & jnp.uint32(0xFFFF0000), F32)
    return a, b


def _adaln_kernel(c_ref, w_ref, b_ref, o_ref):
    c = c_ref[...]
    a = c * jax.nn.sigmoid(c)
    o_ref[...] = jnp.dot(a, w_ref[...], preferred_element_type=F32, precision=HIGHEST) + b_ref[...]


def _adaln(c, w, b, tn=512):
    rows, d = c.shape
    n = w.shape[1]
    return pl.pallas_call(
        _adaln_kernel,
        grid=(n // tn,),
        in_specs=[pl.BlockSpec((rows, d), lambda j: (0, 0)),
                  pl.BlockSpec((d, tn), lambda j: (0, j)),
                  pl.BlockSpec((1, tn), lambda j: (0, j))],
        out_specs=pl.BlockSpec((rows, tn), lambda j: (0, j)),
        out_shape=jax.ShapeDtypeStruct((rows, n), F32),
        compiler_params=_params(("arbitrary",)),
        name="adaln",
    )(c, w, b.reshape(1, n))


def _lam_kernel(q1_ref, k1_ref, q2_ref, k2_ref, o_ref, *, lam_init):
    a = jnp.sum(q1_ref[...] * k1_ref[...], axis=-1, keepdims=True)
    b = jnp.sum(q2_ref[...] * k2_ref[...], axis=-1, keepdims=True)
    lam = jnp.exp(a) - jnp.exp(b) + lam_init
    o_ref[...] = jnp.broadcast_to(lam, o_ref.shape)


def _lam(q1, k1, q2, k2, lam_init):
    args = [a.reshape(1, HEAD_DIM) for a in (q1, k1, q2, k2)]
    return pl.pallas_call(
        functools.partial(_lam_kernel, lam_init=lam_init),
        out_shape=jax.ShapeDtypeStruct((8, LANES), F32),
        name="lam",
    )(*args)


def _inproj_kernel(x_ref, shift_ref, scale_ref, g1_ref, win_ref, rc_ref, ra_ref, rb_ref,
                   wsp_ref, bsp_ref, gln_ref, bln_ref, wbb_ref,
                   q_ref, kf_ref, vf_ref, kb_ref, vt_ref, sga_ref, gyb_ref, *rest,
                   chunked, emit_gv):
    if emit_gv:
        gv_ref, sg_scr = rest
    else:
        (sg_scr,) = rest
    tm = x_ref.shape[0]
    x = x_ref[...]
    h = _rmsnorm(x, g1_ref[...]) * (1.0 + scale_ref[...]) + shift_ref[...]
    hb = h.astype(BF16)

    def proj(c0, n):
        return jnp.dot(hb, win_ref[:, c0:c0 + n], preferred_element_type=F32)

    rc, ra, rb = rc_ref[...], ra_ref[...], rb_ref[...]

    def rope(z):
        return z * rc + pltpu.roll(z, LANES - ROPE_DIM // 2, 1) * ra + pltpu.roll(z, ROPE_DIM // 2, 1) * rb

    zq = proj(COL_Q, Q_COLS)
    for j in range(Q_COLS // LANES):
        z = rope(zq[:, j * LANES:(j + 1) * LANES])
        q_ref[:, j * LANES:(j + 1) * LANES] = (z * Q_SCALE).astype(BF16)
    zk = proj(COL_K, K_COLS)
    for j in range(N_KV_HEADS):
        z = rope(zk[:, j * LANES:(j + 1) * LANES])
        kf_ref[pl.ds(j, tm, stride=N_KV_HEADS), :] = z
        kb_ref[:, j * LANES:(j + 1) * LANES] = z.astype(BF16)
    zv = proj(COL_V, V_COLS)
    for j in range(N_KV_HEADS):
        vf_ref[pl.ds(j, tm, stride=N_KV_HEADS), :] = zv[:, j * V_DIM:(j + 1) * V_DIM]
    vt_ref[...] = zv.T.astype(BF16)

    u = jax.nn.gelu(proj(COL_GU, D_MODEL))
    gv = jax.nn.gelu(proj(COL_GV, D_MODEL))
    xc = gv - jnp.mean(gv, axis=-1, keepdims=True)
    vr = xc * lax.rsqrt(jnp.mean(xc * xc, axis=-1, keepdims=True) + EPS) * gln_ref[...] + bln_ref[...]
    if emit_gv:
        gv_ref[...] = vr
    if chunked:
        vr16 = vr.astype(BF16)
        row = lax.broadcasted_iota(I32, (CHUNK, CHUNK), 0)
        col = lax.broadcasted_iota(I32, (CHUNK, CHUNK), 1)
        for g in range(GMLP_GROUPS):
            w = jnp.where(row >= col, wsp_ref[g], 0.0).astype(BF16)
            gs = slice(g * GMLP_GROUP_DIM, (g + 1) * GMLP_GROUP_DIM)
            for c in range(tm // CHUNK):
                rs = slice(c * CHUNK, (c + 1) * CHUNK)
                sp = jnp.dot(w, vr16[rs, gs], preferred_element_type=F32) + bsp_ref[g]
                sg_scr[rs, gs] = (u[rs, gs] * sp).astype(BF16)
    else:
        sg_scr[...] = (u * (vr * wsp_ref[...] + bsp_ref[...])).astype(BF16)
    yb = jnp.dot(sg_scr[...], wbb_ref[...], preferred_element_type=F32)
    gyb_ref[...] = (jax.nn.sigmoid(proj(COL_GB, D_MODEL)) * yb).astype(BF16)
    sga_ref[...] = jax.nn.sigmoid(proj(COL_GA, D_MODEL)).astype(BF16)


def _inproj(x3, shift, scale, g1, win, rope_tabs, wsp, bsp, gln, bln, wbb, *, tm, chunked, emit_gv):
    B, S, D = x3.shape
    per_token = shift.shape[1] != 1
    nt = S // tm
    mod_spec = (pl.BlockSpec((None, tm, D), lambda b, i: (b, i, 0)) if per_token
                else pl.BlockSpec((None, 1, D), lambda b, i: (b, 0, 0)))
    rope_rows = rope_tabs[0].shape[0]
    rope_spec = (pl.BlockSpec((tm, LANES), lambda b, i: (i, 0)) if rope_rows != 1
                 else pl.BlockSpec((1, LANES), lambda b, i: (0, 0)))
    const2 = lambda a: pl.BlockSpec(a.shape, lambda b, i: (0,) * a.ndim, pipeline_mode=pl.Buffered(1))
    tok = lambda n: pl.BlockSpec((None, tm, n), lambda b, i: (b, i, 0))
    head_rows = pl.BlockSpec((None, tm * N_KV_HEADS, LANES), lambda b, i: (b, i, 0))
    out_shapes = [jax.ShapeDtypeStruct((B, S, Q_COLS), BF16),
                  jax.ShapeDtypeStruct((B, S * N_KV_HEADS, LANES), F32),
                  jax.ShapeDtypeStruct((B, S * N_KV_HEADS, V_DIM), F32),
                  jax.ShapeDtypeStruct((B, S, K_COLS), BF16),
                  jax.ShapeDtypeStruct((B, V_COLS, S), BF16),
                  jax.ShapeDtypeStruct((B, S, D), BF16),
                  jax.ShapeDtypeStruct((B, S, D), BF16)]
    out_specs = [tok(Q_COLS), head_rows, head_rows, tok(K_COLS),
                 pl.BlockSpec((None, V_COLS, tm), lambda b, i: (b, 0, i)), tok(D), tok(D)]
    if emit_gv:
        out_shapes.append(jax.ShapeDtypeStruct((B, S, D), F32))
        out_specs.append(tok(D))
    return pl.pallas_call(
        functools.partial(_inproj_kernel, chunked=chunked, emit_gv=emit_gv),
        grid=(B, nt),
        in_specs=[tok(D), mod_spec, mod_spec, const2(g1), const2(win),
                  rope_spec, rope_spec, rope_spec, const2(wsp), const2(bsp),
                  const2(gln), const2(bln), const2(wbb)],
        out_specs=out_specs,
        out_shape=out_shapes,
        scratch_shapes=[pltpu.VMEM((tm, D), BF16)],
        compiler_params=_params(("arbitrary", "arbitrary")),
        name="inproj",
    )(x3, shift, scale, g1, win, *rope_tabs, wsp, bsp, gln, bln, wbb)


ATTN_COL_SPLIT = 2


def _attn_prompt_kernel(lam_ref, q_ref, k_ref, vt_ref, gsub_ref, o_ref, m_scr, l_scr, acc_scr,
                        *, tq, out_scale):
    i = pl.program_id(2)
    q = q_ref[...]
    lane = lax.broadcasted_iota(I32, (tq, LANES), 1)
    lo = lane < HEAD_DIM
    zero = jnp.zeros((tq, LANES), BF16)
    heads = [q[:, g * LANES:(g + 1) * LANES] for g in range(GQA)]
    pw = tq // ATTN_COL_SPLIT
    masked_heads = [jnp.where(lo, h, zero) for h in heads] + [jnp.where(lo, zero, h) for h in heads]
    qp = [h[c * pw:(c + 1) * pw] for h in masked_heads for c in range(ATTN_COL_SPLIT)]
    n_pieces = len(qp)
    m_scr[...] = jnp.full(m_scr.shape, NEG_INF, F32)
    l_scr[...] = jnp.zeros(l_scr.shape, F32)
    acc_scr[...] = jnp.zeros(acc_scr.shape, F32)

    def step(j, masked):
        start = pl.multiple_of(j * tq, tq)
        k = k_ref[pl.ds(start, tq), :]
        vt = vt_ref[:, pl.ds(start, tq)]
        scores = [lax.dot_general(k, qp[p], (((1,), (1,)), ((), ())), preferred_element_type=F32)
                  for p in range(n_pieces)]
        for p in range(n_pieces):
            s = scores[p]
            if masked:
                kpos = lax.broadcasted_iota(I32, (tq, pw), 0)
                qpos = lax.broadcasted_iota(I32, (tq, pw), 1) + (p % ATTN_COL_SPLIT) * pw
                s = jnp.where(qpos >= kpos, s, NEG_INF)
            m_prev = m_scr[p]
            m_new = jnp.maximum(m_prev, jnp.max(s, axis=0, keepdims=True))
            alpha = jnp.exp2(m_prev - m_new)
            e = jnp.exp2(s - m_new)
            l_scr[p] = alpha * l_scr[p] + jnp.sum(e, axis=0, keepdims=True)
            acc_scr[p] = alpha * acc_scr[p] + jnp.dot(vt, e.astype(BF16), preferred_element_type=F32)
            m_scr[p] = m_new

    def body(j, carry):
        step(j, False)
        return carry

    lax.fori_loop(0, i, body, 0)
    step(i, True)

    lam = lam_ref[0, 0]
    for g in range(GQA):
        for c in range(ATTN_COL_SPLIT):
            p1 = g * ATTN_COL_SPLIT + c
            p2 = (GQA + g) * ATTN_COL_SPLIT + c
            ot = acc_scr[p1] / l_scr[p1] - lam * (acc_scr[p2] / l_scr[p2])
            ot = ot * lax.rsqrt(jnp.mean(ot * ot, axis=0, keepdims=True) + EPS) * gsub_ref[...] * out_scale
            o_ref[c * pw:(c + 1) * pw, g * V_DIM:(g + 1) * V_DIM] = ot.T.astype(o_ref.dtype)


def _attn_prompt(lam, q, kb, vt, gsub_col, *, tq, out_scale):
    B, S, _ = q.shape
    n_pieces = 2 * GQA * ATTN_COL_SPLIT
    pw = tq // ATTN_COL_SPLIT
    return pl.pallas_call(
        functools.partial(_attn_prompt_kernel, tq=tq, out_scale=out_scale),
        grid=(B, N_KV_HEADS, S // tq),
        in_specs=[pl.BlockSpec(memory_space=pltpu.SMEM),
                  pl.BlockSpec((None, tq, GQA * LANES), lambda b, n, i: (b, i, n)),
                  pl.BlockSpec((None, S, LANES), lambda b, n, i: (b, 0, n)),
                  pl.BlockSpec((None, V_DIM, S), lambda b, n, i: (b, n, 0)),
                  pl.BlockSpec((V_DIM, 1), lambda b, n, i: (0, 0))],
        out_specs=pl.BlockSpec((None, tq, GQA * V_DIM), lambda b, n, i: (b, i, n)),
        out_shape=jax.ShapeDtypeStruct((B, S, N_HEADS * V_DIM), BF16),
        scratch_shapes=[pltpu.VMEM((n_pieces, 1, pw), F32), pltpu.VMEM((n_pieces, 1, pw), F32),
                        pltpu.VMEM((n_pieces, V_DIM, pw), F32)],
        compiler_params=_params(("arbitrary", "arbitrary", "arbitrary")),
        name="attn_prompt",
    )(lam, q, kb, vt, gsub_col)


N_QROWS = 2 * N_HEADS
PAGE_ROWS = PAGE_SIZE * N_KV_HEADS


def _attn_sample_kernel(pt_ref, lam_ref, q_ref, kn_ref, vn_ref, gsub_ref, ck_ref, cv_ref, o_ref,
                        kbuf, vbuf, sems, m_scr, l_scr, acc_scr, *, pages, out_scale):
    b = pl.program_id(0)
    c = pl.program_id(1)
    n_c = pl.num_programs(1)
    step = b * n_c + c
    n_steps = pl.num_programs(0) * n_c
    slot = lax.rem(step, 2)

    def page_copies(bb, cc, sl):
        out = []
        for p in range(pages):
            page = pt_ref[bb, cc * pages + p]
            out.append(pltpu.make_async_copy(ck_ref.at[page], kbuf.at[sl, p], sems.at[sl]))
            out.append(pltpu.make_async_copy(cv_ref.at[page], vbuf.at[sl, p], sems.at[sl]))
        return out

    def start_all(copies):
        for n, cp in enumerate(copies):
            cp.start(priority=n % 2)

    @pl.when(step == 0)
    def _():
        start_all(page_copies(0, 0, 0))

    @pl.when(step + 1 < n_steps)
    def _():
        last_c = c + 1 == n_c
        start_all(page_copies(jnp.where(last_c, b + 1, b), jnp.where(last_c, 0, c + 1), 1 - slot))

    for cp in page_copies(b, c, slot):
        cp.wait()

    @pl.when(c == 0)
    def _():
        m_scr[...] = jnp.full(m_scr.shape, NEG_INF, F32)
        l_scr[...] = jnp.zeros(l_scr.shape, F32)
        acc_scr[...] = jnp.zeros(acc_scr.shape, F32)

    q = q_ref[...]
    r = lax.broadcasted_iota(I32, (N_QROWS, PAGE_ROWS), 0)
    col = lax.broadcasted_iota(I32, (N_QROWS, PAGE_ROWS), 1)
    valid = (col & (N_KV_HEADS - 1)) == ((r & (N_HEADS - 1)) >> 1)
    k_pages = [kbuf[slot, p] for p in range(pages)]
    v_pages = [vbuf[slot, p] for p in range(pages)]
    scores = []
    for p in range(pages):
        s = lax.dot_general(q, k_pages[p].astype(BF16), (((1,), (1,)), ((), ())),
                            preferred_element_type=F32)
        scores.append(jnp.where(valid, s, NEG_INF))
    m_prev = m_scr[...]
    m_new = m_prev
    for s in scores:
        m_new = jnp.maximum(m_new, jnp.max(s, axis=-1, keepdims=True))
    alpha = jnp.exp2(m_prev - m_new)
    l_new = alpha * l_scr[...]
    acc = alpha * acc_scr[...]
    for p in range(pages):
        e = jnp.where(valid, jnp.exp2(scores[p] - m_new), 0.0)
        l_new = l_new + jnp.sum(e, axis=-1, keepdims=True)
        acc = acc + jnp.dot(e.astype(BF16), v_pages[p].astype(BF16), preferred_element_type=F32)
    m_scr[...] = m_new
    l_scr[...] = l_new
    acc_scr[...] = acc

    @pl.when(c == pl.num_programs(1) - 1)
    def _():
        s_self = jnp.sum(q.astype(F32) * kn_ref[...], axis=-1, keepdims=True)
        m_fin = jnp.maximum(m_new, s_self)
        a = jnp.exp2(m_new - m_fin)
        e_self = jnp.exp2(s_self - m_fin)
        o = (a * acc + e_self * vn_ref[...]) / (a * l_new + e_self)
        lam = lam_ref[0, 0]
        od = o[:N_HEADS] - lam * o[N_HEADS:]
        o_ref[...] = (_rmsnorm(od, gsub_ref[...]) * out_scale).astype(o_ref.dtype)


def _attn_sample(page_table, lam, q16, kn16, vn16, gsub, ck3, cv3, *, pages, out_scale):
    DB, n_pages = page_table.shape
    steps = n_pages // pages
    row3 = lambda n: pl.BlockSpec((None, n, LANES), lambda b, c, pt: (b, 0, 0))
    grid_spec = pltpu.PrefetchScalarGridSpec(
        num_scalar_prefetch=1,
        grid=(DB, steps),
        in_specs=[pl.BlockSpec(memory_space=pltpu.SMEM), row3(N_QROWS), row3(N_QROWS), row3(N_QROWS),
                  pl.BlockSpec((1, V_DIM), lambda b, c, pt: (0, 0)),
                  pl.BlockSpec(memory_space=pl.ANY), pl.BlockSpec(memory_space=pl.ANY)],
        out_specs=row3(N_HEADS),
        scratch_shapes=[pltpu.VMEM((2, pages, PAGE_ROWS, LANES), F32),
                        pltpu.VMEM((2, pages, PAGE_ROWS, LANES), F32),
                        pltpu.SemaphoreType.DMA((2,)),
                        pltpu.VMEM((N_QROWS, 1), F32), pltpu.VMEM((N_QROWS, 1), F32),
                        pltpu.VMEM((N_QROWS, V_DIM), F32)],
    )
    return pl.pallas_call(
        functools.partial(_attn_sample_kernel, pages=pages, out_scale=out_scale),
        grid_spec=grid_spec,
        out_shape=jax.ShapeDtypeStruct((DB, N_HEADS, V_DIM), BF16),
        compiler_params=_params(("arbitrary", "arbitrary")),
        name="attn_sample",
    )(page_table, lam, q16, kn16, vn16, gsub, ck3, cv3)


def _post_kernel(o_ref, sga_ref, gyb_ref, x_ref, gate_ref, shift_ref, scale_ref, g2_ref,
                 wba_ref, wout_ref, wr_ref, br_ref,
                 x1_ref, h2p_ref, route_ref, counts_ref, base_scr):
    tm = x_ref.shape[0]
    first = (pl.program_id(0) == 0) & (pl.program_id(1) == 0)

    @pl.when(first)
    def _():
        base_scr[...] = jnp.zeros(base_scr.shape, F32)

    ya = jnp.dot(o_ref[...], wba_ref[...], preferred_element_type=F32)
    merged = sga_ref[...].astype(F32) * ya + gyb_ref[...].astype(F32)
    upd = jnp.dot(merged.astype(BF16), wout_ref[...], preferred_element_type=F32)
    x1 = x_ref[...] + gate_ref[...] * upd
    x1_ref[...] = x1
    h2 = _rmsnorm(x1, g2_ref[...]) * (1.0 + scale_ref[...]) + shift_ref[...]
    half = D_MODEL // 2
    h2p_ref[...] = _pack_pair(h2[:, :half], h2[:, half:])

    h_hi = h2.astype(BF16)
    h_lo = (h2 - h_hi.astype(F32)).astype(BF16)
    logits = (jnp.dot(h_hi, wr_ref[0], preferred_element_type=F32)
              + (jnp.dot(h_hi, wr_ref[1], preferred_element_type=F32)
                 + jnp.dot(h_lo, wr_ref[0], preferred_element_type=F32))) + br_ref[...]
    lane = lax.broadcasted_iota(I32, (tm, LANES), 1)
    lanef = lane.astype(F32)
    big = float(LANES)
    is_g = lane < N_GROUPS
    lg = jnp.where(is_g, logits, -jnp.inf)
    gmax = jnp.max(lg, axis=-1, keepdims=True)
    g_idx = jnp.min(jnp.where(lg == gmax, lanef, big), axis=-1, keepdims=True)
    p_grp = 1.0 / jnp.sum(jnp.where(is_g, jnp.exp(logits - gmax), 0.0), axis=-1, keepdims=True)
    e_lo = N_GROUPS + EXPERTS_PER_GROUP * g_idx
    in_grp = (lanef >= e_lo) & (lanef < e_lo + EXPERTS_PER_GROUP)
    emax = jnp.max(jnp.where(in_grp, logits, -jnp.inf), axis=-1, keepdims=True)
    ee = jnp.where(in_grp, jnp.exp(logits - emax), 0.0)
    pe = ee / jnp.sum(ee, axis=-1, keepdims=True)
    pe_m = jnp.where(in_grp, pe, -1.0)
    v1 = jnp.max(pe_m, axis=-1, keepdims=True)
    i1 = jnp.min(jnp.where(pe_m == v1, lanef, big), axis=-1, keepdims=True)
    pe_m2 = jnp.where(lanef == i1, -1.0, pe_m)
    v2 = jnp.max(pe_m2, axis=-1, keepdims=True)
    i2 = jnp.min(jnp.where(pe_m2 == v2, lanef, big), axis=-1, keepdims=True)
    tsum = v1 + v2
    w1 = p_grp * (v1 / tsum)
    w2 = p_grp * (v2 / tsum)
    e1 = i1 - N_GROUPS
    e2 = i2 - N_GROUPS

    oh1 = (lanef == e1).astype(F32)
    oh2 = (lanef == e2).astype(F32)
    ohs = oh1 + oh2
    rr = lax.broadcasted_iota(I32, (tm, tm), 0)
    cc = lax.broadcasted_iota(I32, (tm, tm), 1)
    tri = jnp.where(rr > cc, 1.0, 0.0).astype(BF16)
    before = jnp.dot(tri, ohs.astype(BF16), preferred_element_type=F32) + base_scr[...]
    r1 = jnp.sum(before * oh1, axis=-1, keepdims=True)
    r2 = jnp.sum(before * oh2, axis=-1, keepdims=True)
    new_base = base_scr[...] + jnp.sum(ohs, axis=0, keepdims=True)
    base_scr[...] = new_base
    counts_ref[...] = jnp.broadcast_to(new_base, counts_ref.shape)

    route = jnp.zeros((tm, LANES), F32)
    for idx, val in enumerate((e1, e2, r1, r2, w1, w2)):
        route = jnp.where(lane == idx, val, route)
    route_ref[...] = route


def _post(o, sga, gyb, x3, gate, shift, scale, g2, wba, wout, wr, br, *, tm):
    B, S, D = x3.shape
    per_token = gate.shape[1] != 1
    mod_spec = (pl.BlockSpec((None, tm, D), lambda b, i: (b, i, 0)) if per_token
                else pl.BlockSpec((None, 1, D), lambda b, i: (b, 0, 0)))
    const2 = lambda a: pl.BlockSpec(a.shape, lambda b, i: (0,) * a.ndim, pipeline_mode=pl.Buffered(1))
    tok = lambda n: pl.BlockSpec((None, tm, n), lambda b, i: (b, i, 0))
    return pl.pallas_call(
        _post_kernel,
        grid=(B, S // tm),
        in_specs=[tok(D), tok(D), tok(D), tok(D), mod_spec, mod_spec, mod_spec, const2(g2),
                  const2(wba), const2(wout), const2(wr), const2(br)],
        out_specs=[tok(D), tok(D // 2), tok(LANES), pl.BlockSpec((8, LANES), lambda b, i: (0, 0))],
        out_shape=[jax.ShapeDtypeStruct((B, S, D), F32),
                   jax.ShapeDtypeStruct((B, S, D // 2), U32),
                   jax.ShapeDtypeStruct((B, S, LANES), F32),
                   jax.ShapeDtypeStruct((8, LANES), F32)],
        scratch_shapes=[pltpu.VMEM((1, LANES), F32)],
        compiler_params=_params(("arbitrary", "arbitrary")),
        name="post",
    )(o, sga, gyb, x3, gate, shift, scale, g2, wba, wout, wr, br)


ROW_UNROLL = 8


def _row_copy(src, dst, src_row, dst_row, sem):
    return pltpu.make_async_copy(src.at[pl.ds(src_row, 1)], dst.at[pl.ds(dst_row, 1)], sem)


def _dispatch_kernel(dest_ref, h_ref, xs_in_ref, xs_ref, dest_smem, sem, dsem):
    del xs_in_ref
    tm = h_ref.shape[0]
    cp = pltpu.make_async_copy(dest_ref.at[0], dest_smem, dsem)
    cp.start()
    cp.wait()

    def issue(blk, carry):
        for u in range(ROW_UNROLL):
            r = blk * ROW_UNROLL + u
            _row_copy(h_ref, xs_ref, r, dest_smem[0, 2 * r], sem).start(priority=0)
            _row_copy(h_ref, xs_ref, r, dest_smem[0, 2 * r + 1], sem).start(priority=1)
        return carry

    lax.fori_loop(0, tm // ROW_UNROLL, issue, 0)
    for _ in range(2):
        pltpu.make_async_copy(h_ref, xs_ref.at[pl.ds(0, tm)], sem).wait()


def _dispatch(dest, h2p, slots, *, tm):
    T, W = h2p.shape
    nt = T // tm
    xs0 = jnp.zeros((slots, W), U32)
    return pl.pallas_call(
        _dispatch_kernel,
        grid=(nt,),
        in_specs=[pl.BlockSpec((1, 1, 2 * tm), lambda i: (i, 0, 0)),
                  pl.BlockSpec((tm, W), lambda i: (i, 0)),
                  pl.BlockSpec(memory_space=pl.ANY)],
        out_specs=pl.BlockSpec(memory_space=pl.ANY),
        out_shape=jax.ShapeDtypeStruct((slots, W), U32),
        scratch_shapes=[pltpu.SMEM((1, 2 * tm), I32), pltpu.SemaphoreType.DMA(()), pltpu.SemaphoreType.DMA(())],
        input_output_aliases={2: 0},
        compiler_params=_params(("arbitrary",)),
        name="dispatch",
    )(dest.reshape(nt, 1, 2 * tm), h2p, xs0)


SC_GATHER_ROWS = 64


def _sc_gather(table, idx):
    info = plsc.get_sparse_core_info()
    n_workers = info.num_cores * info.num_subcores
    B = idx.shape[0]
    W = table.shape[1]
    R = SC_GATHER_ROWS
    per_worker = B // n_workers
    n_chunks = per_worker // R
    assert per_worker * n_workers == B and n_chunks * R == per_worker and n_chunks % 2 == 0
    mesh = plsc.VectorSubcoreMesh(core_axis_name="c", subcore_axis_name="s")

    def body(table_hbm, idx_hbm, out_hbm, idx_v, rows_v, sems):
        wid = lax.axis_index("s") * info.num_cores + lax.axis_index("c")
        base = wid * per_worker

        def gather(chunk, slot):
            off = pl.multiple_of(base + chunk * R, 8)
            pltpu.sync_copy(idx_hbm.at[pl.ds(off, R)], idx_v.at[slot])
            return pltpu.make_async_copy(table_hbm.at[idx_v.at[slot]], rows_v.at[slot], sems.at[slot])

        def flush(chunk, slot):
            off = pl.multiple_of(base + chunk * R, 8)
            pltpu.sync_copy(rows_v.at[slot], out_hbm.at[pl.ds(off, R)])

        gather(0, 0).start()

        @pl.loop(0, n_chunks, step=2)
        def _(g):
            gather(g + 1, 1).start()
            pltpu.make_async_copy(table_hbm.at[idx_v.at[0]], rows_v.at[0], sems.at[0]).wait()
            flush(g, 0)

            @pl.when(g + 2 < n_chunks)
            def _():
                gather(g + 2, 0).start()

            pltpu.make_async_copy(table_hbm.at[idx_v.at[1]], rows_v.at[1], sems.at[1]).wait()
            flush(g + 1, 1)

    return pl.kernel(
        body,
        out_type=jax.ShapeDtypeStruct((B, W), I32),
        mesh=mesh,
        scratch_types=[pltpu.VMEM((2, R), I32), pltpu.VMEM((2, R, W), I32), pltpu.SemaphoreType.DMA((2,))],
        name="sc_gather",
    )(table, idx)


def _ffn_kernel(be_ref, nu_ref, x_ref, wg_ref, wu_ref, wd_ref, y_ref):
    del be_ref
    i = pl.program_id(0)

    @pl.when(i < nu_ref[0])
    def _():
        a, b = _unpack_pair(x_ref[...])
        x = jnp.concatenate([a.astype(BF16), b.astype(BF16)], axis=1)
        hg = jnp.dot(x, wg_ref[...], preferred_element_type=F32)
        hu = jnp.dot(x, wu_ref[...], preferred_element_type=F32)
        hid = (hg * jax.nn.sigmoid(hg)) * hu
        y = jnp.dot(hid.astype(BF16), wd_ref[...], preferred_element_type=F32)
        half = D_MODEL // 2
        y_ref[...] = _pack_pair(y[:, :half], y[:, half:])

    @pl.when(i >= nu_ref[0])
    def _():
        y_ref[...] = jnp.zeros(y_ref.shape, U32)


def _ffn(block_expert, n_used, xs, weg, weu, wed, *, bm):
    slots, W = xs.shape
    nb = slots // bm
    grid_spec = pltpu.PrefetchScalarGridSpec(
        num_scalar_prefetch=2,
        grid=(nb,),
        in_specs=[pl.BlockSpec((bm, W), lambda i, be, nu: (i, 0)),
                  pl.BlockSpec((None, D_MODEL, D_EXPERT), lambda i, be, nu: (be[i], 0, 0)),
                  pl.BlockSpec((None, D_MODEL, D_EXPERT), lambda i, be, nu: (be[i], 0, 0)),
                  pl.BlockSpec((None, D_EXPERT, D_MODEL), lambda i, be, nu: (be[i], 0, 0))],
        out_specs=pl.BlockSpec((bm, W), lambda i, be, nu: (i, 0)),
    )
    return pl.pallas_call(
        _ffn_kernel,
        grid_spec=grid_spec,
        out_shape=jax.ShapeDtypeStruct((slots, W), U32),
        compiler_params=_params(("arbitrary",)),
        name="ffn",
    )(block_expert, n_used, xs, weg, weu, wed)


def _combine_kernel(dest_ref, dest_next_ref, ys_ref, x1_ref, route_ref, gate_ref, fshift_ref, fscale_ref,
                    gf_ref, o_ref, dest_smem, rows_scr, sems, dsem):
    tm = x1_ref.shape[0]
    step = pl.program_id(0) * pl.num_programs(1) + pl.program_id(1)
    n_steps = pl.num_programs(0) * pl.num_programs(1)
    slot = lax.rem(step, 2)

    def gather(dref, sl):
        cp = pltpu.make_async_copy(dref.at[0], dest_smem, dsem)
        cp.start()
        cp.wait()

        def issue(blk, carry):
            for u in range(ROW_UNROLL):
                r = blk * ROW_UNROLL + u
                _row_copy(ys_ref, rows_scr.at[sl, 0], dest_smem[0, 2 * r], r, sems.at[sl]).start(priority=0)
                _row_copy(ys_ref, rows_scr.at[sl, 1], dest_smem[0, 2 * r + 1], r, sems.at[sl]).start(priority=1)
            return carry

        lax.fori_loop(0, tm // ROW_UNROLL, issue, 0)

    @pl.when(step == 0)
    def _():
        gather(dest_ref, 0)

    @pl.when(step + 1 < n_steps)
    def _():
        gather(dest_next_ref, 1 - slot)

    for half in range(2):
        pltpu.make_async_copy(ys_ref.at[pl.ds(0, tm)], rows_scr.at[slot, half], sems.at[slot]).wait()

    o_ref[...] = _combine_math(rows_scr[slot, 0], rows_scr[slot, 1], x1_ref[...], route_ref[...],
                               gate_ref[...], fshift_ref[...], fscale_ref[...], gf_ref[...])


def _combine_math(rows1, rows2, x1, route, gate, fshift, fscale, gf):
    w1 = route[:, 4:5]
    w2 = route[:, 5:6]
    a1, b1 = _unpack_pair(rows1)
    a2, b2 = _unpack_pair(rows2)
    y = jnp.concatenate([a1 * w1 + a2 * w2, b1 * w1 + b2 * w2], axis=1)
    x2 = x1 + gate * y
    return _rmsnorm(x2, gf) * (1.0 + fscale) + fshift


def _combine_dense_kernel(rows_ref, x1_ref, route_ref, gate_ref, fshift_ref, fscale_ref, gf_ref, o_ref):
    o_ref[...] = _combine_math(rows_ref[0], rows_ref[1], x1_ref[...], route_ref[...],
                               gate_ref[...], fshift_ref[...], fscale_ref[...], gf_ref[...])


def _combine_dense(rows, x1, route, gate, fshift, fscale, gf, *, tm):
    B, S, D = x1.shape
    W = rows.shape[-1]
    per_token = gate.shape[1] != 1
    mod_spec = (pl.BlockSpec((None, tm, D), lambda b, i: (b, i, 0)) if per_token
                else pl.BlockSpec((None, 1, D), lambda b, i: (b, 0, 0)))
    tok = lambda n: pl.BlockSpec((None, tm, n), lambda b, i: (b, i, 0))
    return pl.pallas_call(
        _combine_dense_kernel,
        grid=(B, S // tm),
        in_specs=[pl.BlockSpec((2, None, tm, W), lambda b, i: (0, b, i, 0)),
                  tok(D), tok(LANES), mod_spec, mod_spec, mod_spec,
                  pl.BlockSpec(gf.shape, lambda b, i: (0, 0))],
        out_specs=tok(D),
        out_shape=jax.ShapeDtypeStruct((B, S, D), F32),
        compiler_params=_params(("arbitrary", "arbitrary")),
        name="combine_dense",
    )(rows, x1, route, gate, fshift, fscale, gf)


def _combine(dest, ys, x1, route, gate, fshift, fscale, gf, *, tm):
    B, S, D = x1.shape
    nt = S // tm
    W = ys.shape[1]
    per_token = gate.shape[1] != 1
    mod_spec = (pl.BlockSpec((None, tm, D), lambda b, i: (b, i, 0)) if per_token
                else pl.BlockSpec((None, 1, D), lambda b, i: (b, 0, 0)))
    tok = lambda n: pl.BlockSpec((None, tm, n), lambda b, i: (b, i, 0))
    dest3 = dest.reshape(B * nt, 1, 2 * tm)
    return pl.pallas_call(
        _combine_kernel,
        grid=(B, nt),
        in_specs=[pl.BlockSpec((1, 1, 2 * tm), lambda b, i: (b * nt + i, 0, 0)),
                  pl.BlockSpec((1, 1, 2 * tm), lambda b, i: (jnp.minimum(b * nt + i + 1, B * nt - 1), 0, 0)),
                  pl.BlockSpec(memory_space=pl.ANY),
                  tok(D), tok(LANES), mod_spec, mod_spec, mod_spec,
                  pl.BlockSpec(gf.shape, lambda b, i: (0, 0))],
        out_specs=tok(D),
        out_shape=jax.ShapeDtypeStruct((B, S, D), F32),
        scratch_shapes=[pltpu.SMEM((1, 2 * tm), I32), pltpu.VMEM((2, 2, tm, W), U32),
                        pltpu.SemaphoreType.DMA((2,)), pltpu.SemaphoreType.DMA(())],
        compiler_params=_params(("arbitrary", "arbitrary")),
        name="combine",
    )(dest3, dest3, ys, x1, route, gate, fshift, fscale, gf)


def _rope_tables(pos):
    half = ROPE_DIM // 2
    inv_freq = ROPE_THETA ** (-jnp.arange(0, ROPE_DIM, 2, dtype=F32) / ROPE_DIM)
    ang = pos.astype(F32)[:, None] * inv_freq[None, :]
    cos, sin = jnp.cos(ang), jnp.sin(ang)
    n = pos.shape[0]
    rest = HEAD_DIM - ROPE_DIM
    c64 = jnp.concatenate([cos, cos, jnp.ones((n, rest), F32)], axis=1)
    a64 = jnp.concatenate([-sin, jnp.zeros((n, HEAD_DIM - half), F32)], axis=1)
    b64 = jnp.concatenate([jnp.zeros((n, half), F32), sin, jnp.zeros((n, rest), F32)], axis=1)
    return tuple(jnp.tile(t, (1, LANES // HEAD_DIM)) for t in (c64, a64, b64))


def _moe_plan(route, counts, bm):
    T = route.shape[0]
    e = route[:, 0:2].astype(I32)
    rank = route[:, 2:4].astype(I32)
    cnt = counts[0, :N_EXPERTS].astype(I32)
    padded = (cnt + bm - 1) // bm * bm
    pad_end = jnp.cumsum(padded)
    pad_start = pad_end - padded
    ids = jnp.arange(N_EXPERTS, dtype=I32)
    dest = rank + jnp.sum(jnp.where(e[..., None] == ids, pad_start, 0), axis=-1)
    n_blocks = -(-(2 * T) // bm) + N_EXPERTS
    block_lo = jnp.arange(n_blocks, dtype=I32) * bm
    block_expert = jnp.minimum(jnp.sum((pad_end[None, :] <= block_lo[:, None]).astype(I32), axis=1),
                               N_EXPERTS - 1)
    n_used = (pad_end[-1:] // bm).astype(I32)
    return dest.reshape(-1), block_expert, n_used, n_blocks * bm


def _slot_tokens(route, counts, bm, slots):
    T = route.shape[0]
    e_flat = route[:, 0:2].astype(I32).reshape(-1)
    cnt = counts[0, :N_EXPERTS].astype(I32)
    padded = (cnt + bm - 1) // bm * bm
    pad_end = jnp.cumsum(padded)
    pad_start = pad_end - padded
    start = jnp.cumsum(cnt) - cnt
    order = jnp.argsort(e_flat, stable=True).astype(I32)
    s = jnp.arange(slots, dtype=I32)
    eb = jnp.minimum(jnp.sum((pad_end[None, :] <= s[:, None]).astype(I32), axis=1), N_EXPERTS - 1)
    hot = eb[:, None] == jnp.arange(N_EXPERTS, dtype=I32)
    pick = lambda tab: jnp.sum(jnp.where(hot, tab, 0), axis=1)
    r = s - pick(pad_start)
    valid = r < pick(cnt)
    src = jnp.clip(pick(start) + r, 0, 2 * T - 1)
    return jnp.where(valid, order[src] // 2, s % T)


def _layer(x3, mod, modf, pos_tabs, attend, lam_init, wts, *, tm, tmr, bm, chunked, emit_gv, sc_rows):
    B, S, D = x3.shape
    m = lambda k: mod[:, :, k, :]
    outs = _inproj(x3, m(0), m(1), wts['g1'], wts['win'], pos_tabs, wts['wsp'], wts['bsp'],
                   wts['gln'], wts['bln'], wts['wbb'], tm=tm, chunked=chunked, emit_gv=emit_gv)
    q, kf, vf, kb, vt, sga, gyb = outs[:7]
    gv = outs[7] if emit_gv else None
    o = attend(q, kf, vf, kb, vt)
    x1, h2p, route, counts = _post(o, sga, gyb, x3, m(2), m(3), m(4), wts['g2'], wts['wba'],
                                   wts['wout'], wts['wr'], wts['br'], tm=tm)
    T = B * S
    W = D // 2
    dest, block_expert, n_used, slots = _moe_plan(route.reshape(T, LANES), counts, bm)
    fin = (m(5), modf[:, :, 0, :], modf[:, :, 1, :], wts['gf'])
    if sc_rows:
        as_i32 = lambda a: lax.bitcast_convert_type(a, I32)
        as_u32 = lambda a: lax.bitcast_convert_type(a, U32)
        xs = as_u32(_sc_gather(as_i32(h2p.reshape(T, W)), _slot_tokens(route.reshape(T, LANES), counts, bm, slots)))
        ys = _ffn(block_expert, n_used, xs, wts['weg'], wts['weu'], wts['wed'], bm=bm)
        rows = as_u32(_sc_gather(as_i32(ys), dest.reshape(T, 2).T.reshape(-1))).reshape(2, B, S, W)
        y = _combine_dense(rows, x1, route, *fin, tm=tm)
    else:
        xs = _dispatch(dest, h2p.reshape(T, W), slots, tm=tmr)
        ys = _ffn(block_expert, n_used, xs, wts['weg'], wts['weu'], wts['wed'], bm=bm)
        y = _combine(dest, ys, x1, route, *fin, tm=tmr)
    return y, kf, vf, gv


def kernel(x_prompt, x_sample, cache_k, cache_v, page_table, c_prompt, c_sample, w_ada, b_ada, w_ada_final, b_ada_final, g_norm1, g_norm2, g_final, w_in, lambda_q1, lambda_k1, lambda_q2, lambda_k2, g_subln, g_gmlp_ln, b_gmlp_ln, w_spatial, b_spatial, w_branch_a, w_branch_b, w_out, w_router_group, b_router_group, w_router_expert, b_router_expert, w_exp_gate, w_exp_up, w_exp_down):
    depth = w_ada.shape[0]
    assert depth == 1
    B, S, D = x_prompt.shape
    DB, DS, _ = x_sample.shape
    assert DS == 1 and D == D_MODEL
    n_pages = page_table.shape[1]
    n_past = n_pages * PAGE_SIZE
    l = 0
    lam_init = 0.8 - 0.6 * math.exp(-0.3 * l)
    out_scale = 1.0 - lam_init

    c_all = jnp.concatenate([c_prompt, c_sample], axis=0)
    mod_all = _adaln(c_all, w_ada[l], b_ada[l])
    modf_all = _adaln(c_all, w_ada_final, b_ada_final)
    mod_p = mod_all[:B].reshape(B, 1, 6, D)
    mod_s = mod_all[B:].reshape(1, DB, 6, D)
    modf_p = modf_all[:B].reshape(B, 1, 2, D)
    modf_s = modf_all[B:].reshape(1, DB, 2, D)
    lam = _lam(lambda_q1[l], lambda_k1[l], lambda_q2[l], lambda_k2[l], lam_init)[0:1, 0:1]

    row = lambda a: a.reshape(1, -1)
    wr = jnp.zeros((D, LANES), F32)
    wr = wr.at[:, :N_GROUPS].set(w_router_group[l]).at[:, N_GROUPS:N_GROUPS + N_EXPERTS].set(w_router_expert[l])
    wr_hi = wr.astype(BF16)
    wr_lo = (wr - wr_hi.astype(F32)).astype(BF16)
    br = jnp.zeros((1, LANES), F32)
    br = br.at[0, :N_GROUPS].set(b_router_group[l]).at[0, N_GROUPS:N_GROUPS + N_EXPERTS].set(b_router_expert[l])
    wts = dict(
        g1=row(g_norm1[l]), g2=row(g_norm2[l]), gf=row(g_final), win=w_in[l].astype(BF16),
        gln=row(g_gmlp_ln[l]), bln=row(b_gmlp_ln[l]), wbb=w_branch_b[l].astype(BF16),
        wba=w_branch_a[l].astype(BF16), wout=w_out[l].astype(BF16), wr=jnp.stack([wr_hi, wr_lo]), br=br,
        weg=w_exp_gate[l].astype(BF16), weu=w_exp_up[l].astype(BF16), wed=w_exp_down[l].astype(BF16))
    gsub = row(g_subln[l])

    wts_s = dict(wts, wsp=row(jnp.repeat(w_spatial[l][:, 0, 0], GMLP_GROUP_DIM)),
                 bsp=row(jnp.repeat(b_spatial[l][:, 0], GMLP_GROUP_DIM)))
    tabs_s = _rope_tables(jnp.full((1,), n_past, dtype=I32))
    ck3 = cache_k[l].reshape(-1, PAGE_ROWS, LANES)
    cv3 = cache_v[l].reshape(-1, PAGE_ROWS, LANES)
    half_mask = (jnp.arange(LANES)[None, :] // HEAD_DIM == jnp.arange(2)[:, None])

    def attend_s(q, kf, vf, kb, vb):
        del kb, vb
        qh = q.reshape(DB, 1, N_HEADS, LANES)
        q16 = jnp.where(half_mask[None, :, None, :], qh, jnp.zeros_like(qh)).reshape(DB, N_QROWS, LANES)
        kv_of_row = (jnp.arange(N_QROWS) % N_HEADS) // GQA
        kn16 = kf.reshape(DB, N_KV_HEADS, LANES)[:, kv_of_row]
        vn16 = vf.reshape(DB, N_KV_HEADS, V_DIM)[:, kv_of_row]
        pages = 8 if n_pages % 8 == 0 else (4 if n_pages % 4 == 0 else 1)
        o = _attn_sample(page_table, lam, q16, kn16, vn16, gsub, ck3, cv3, pages=pages, out_scale=out_scale)
        return o.reshape(1, DB, N_HEADS * V_DIM)

    y_s, k_s, v_s, gv_s = _layer(x_sample.reshape(1, DB, D), mod_s, modf_s, tabs_s, attend_s, lam_init,
                                 wts_s, tm=DB, tmr=DB, bm=128, chunked=False, emit_gv=True, sc_rows=False)

    wts_p = dict(wts, wsp=w_spatial[l],
                 bsp=jnp.broadcast_to(b_spatial[l][:, :, None], (GMLP_GROUPS, CHUNK, GMLP_GROUP_DIM)))
    tabs_p = _rope_tables(jnp.arange(S, dtype=I32))
    tq = min(512, S)
    gsub_col = g_subln[l].reshape(V_DIM, 1)
    attend_p = lambda q, kf, vf, kb, vt: _attn_prompt(lam, q, kb, vt, gsub_col, tq=tq, out_scale=out_scale)
    y_p, k_p, v_p, _ = _layer(x_prompt, mod_p, modf_p, tabs_p, attend_p, lam_init, wts_p,
                              tm=min(512, S), tmr=min(256, S), bm=512, chunked=True, emit_gv=False,
                              sc_rows=True)

    return (y_p,
            y_s.reshape(DB, 1, D),
            k_p.reshape(1, B, S, N_KV_HEADS, 2 * HEAD_DIM),
            v_p.reshape(1, B, S, N_KV_HEADS, V_DIM),
            k_s.reshape(1, DB, 1, N_KV_HEADS, 2 * HEAD_DIM),
            v_s.reshape(1, DB, 1, N_KV_HEADS, V_DIM),
            gv_s.reshape(1, DB, 1, D))
```

```python
import functools
import math

import jax
import jax.numpy as jnp
from jax import lax
from jax.experimental import pallas as pl
from jax.experimental.pallas import tpu as pltpu
from jax.experimental.pallas import tpu_sc as plsc

F32 = jnp.float32
BF16 = jnp.bfloat16
U32 = jnp.uint32
I32 = jnp.int32

D_MODEL = 1024
N_HEADS = 8
N_KV_HEADS = 4
GQA = N_HEADS // N_KV_HEADS
HEAD_DIM = 64
V_DIM = 128
ROPE_DIM = 16
ROPE_THETA = 500000.0
PAGE_SIZE = 128
GMLP_GROUPS = 8
GMLP_GROUP_DIM = 128
CHUNK = 128
N_GROUPS = 4
EXPERTS_PER_GROUP = 8
N_EXPERTS = 32
D_EXPERT = 512
Q_COLS = 1024
K_COLS = 512
V_COLS = 512
COL_Q, COL_K, COL_V, COL_GU, COL_GV, COL_GA, COL_GB = 0, 1024, 1536, 2048, 3072, 4096, 5120
EPS = 1e-6
NEG_INF = -1e30
LANES = 128
VMEM_LIMIT = 56 * 1024 * 1024

HIGHEST = lax.Precision.HIGHEST
Q_SCALE = HEAD_DIM ** -0.5 * math.log2(math.e)


def _params(sem, flags=None):
    return pltpu.CompilerParams(dimension_semantics=sem, vmem_limit_bytes=VMEM_LIMIT, flags=flags)


def _rmsnorm(x, g):
    return x * lax.rsqrt(jnp.mean(x * x, axis=-1, keepdims=True) + EPS) * g


def _pack_pair(a, b):
    ua = lax.bitcast_convert_type(a.astype(BF16).astype(F32), U32)
    ub = lax.bitcast_convert_type(b.astype(BF16).astype(F32), U32)
    return lax.bitcast_convert_type((ua >> 16) | (ub & jnp.uint32(0xFFFF0000)), I32)


def _unpack_pair(w):
    w = lax.bitcast_convert_type(w, U32)
    a = lax.bitcast_convert_type(w << 16, F32)
    b = lax.bitcast_convert_type(w & jnp.uint32(0xFFFF0000), F32)
    return a, b


def _adaln_kernel(c_ref, w_ref, b_ref, o_ref):
    c = c_ref[...]
    a = c * jax.nn.sigmoid(c)
    o_ref[...] = jnp.dot(a, w_ref[...], preferred_element_type=F32, precision=HIGHEST) + b_ref[...]


def _adaln(c, w, b, tn=512):
    rows, d = c.shape
    n = w.shape[1]
    return pl.pallas_call(
        _adaln_kernel,
        grid=(n // tn,),
        in_specs=[pl.BlockSpec((rows, d), lambda j: (0, 0)),
                  pl.BlockSpec((d, tn), lambda j: (0, j)),
                  pl.BlockSpec((1, tn), lambda j: (0, j))],
        out_specs=pl.BlockSpec((rows, tn), lambda j: (0, j)),
        out_shape=jax.ShapeDtypeStruct((rows, n), F32),
        compiler_params=_params(("arbitrary",)),
        name="adaln",
    )(c, w, b.reshape(1, n))


def _lam_kernel(q1_ref, k1_ref, q2_ref, k2_ref, o_ref, *, lam_init):
    a = jnp.sum(q1_ref[...] * k1_ref[...], axis=-1, keepdims=True)
    b = jnp.sum(q2_ref[...] * k2_ref[...], axis=-1, keepdims=True)
    lam = jnp.exp(a) - jnp.exp(b) + lam_init
    o_ref[...] = jnp.broadcast_to(lam, o_ref.shape)


def _lam(q1, k1, q2, k2, lam_init):
    args = [a.reshape(1, HEAD_DIM) for a in (q1, k1, q2, k2)]
    return pl.pallas_call(
        functools.partial(_lam_kernel, lam_init=lam_init),
        out_shape=jax.ShapeDtypeStruct((8, LANES), F32),
        name="lam",
    )(*args)


def _inproj_kernel(x_ref, shift_ref, scale_ref, g1_ref, win_ref, rc_ref, ra_ref, rb_ref,
                   wsp_ref, bsp_ref, gln_ref, bln_ref, wbb_ref,
                   q_ref, kf_ref, vf_ref, kb_ref, vt_ref, sga_ref, gyb_ref, *rest,
                   chunked, emit_gv):
    if emit_gv:
        gv_ref, sg_scr = rest
    else:
        (sg_scr,) = rest
    tm = x_ref.shape[0]
    x = x_ref[...]
    h = _rmsnorm(x, g1_ref[...]) * (1.0 + scale_ref[...]) + shift_ref[...]
    hb = h.astype(BF16)

    def proj(c0, n):
        return jnp.dot(hb, win_ref[:, c0:c0 + n], preferred_element_type=F32)

    rc, ra, rb = rc_ref[...], ra_ref[...], rb_ref[...]

    def rope(z):
        return z * rc + pltpu.roll(z, LANES - ROPE_DIM // 2, 1) * ra + pltpu.roll(z, ROPE_DIM // 2, 1) * rb

    zq = proj(COL_Q, Q_COLS)
    for j in range(Q_COLS // LANES):
        z = rope(zq[:, j * LANES:(j + 1) * LANES])
        q_ref[:, j * LANES:(j + 1) * LANES] = (z * Q_SCALE).astype(BF16)
    zk = proj(COL_K, K_COLS)
    for j in range(N_KV_HEADS):
        z = rope(zk[:, j * LANES:(j + 1) * LANES])
        kf_ref[pl.ds(j, tm, stride=N_KV_HEADS), :] = z
        kb_ref[:, j * LANES:(j + 1) * LANES] = z.astype(BF16)
    zv = proj(COL_V, V_COLS)
    for j in range(N_KV_HEADS):
        vf_ref[pl.ds(j, tm, stride=N_KV_HEADS), :] = zv[:, j * V_DIM:(j + 1) * V_DIM]
    vt_ref[...] = zv.T.astype(BF16)

    u = jax.nn.gelu(proj(COL_GU, D_MODEL))
    gv = jax.nn.gelu(proj(COL_GV, D_MODEL))
    xc = gv - jnp.mean(gv, axis=-1, keepdims=True)
    vr = xc * lax.rsqrt(jnp.mean(xc * xc, axis=-1, keepdims=True) + EPS) * gln_ref[...] + bln_ref[...]
    if emit_gv:
        gv_ref[...] = vr
    if chunked:
        vr16 = vr.astype(BF16)
        row = lax.broadcasted_iota(I32, (CHUNK, CHUNK), 0)
        col = lax.broadcasted_iota(I32, (CHUNK, CHUNK), 1)
        for g in range(GMLP_GROUPS):
            w = jnp.where(row >= col, wsp_ref[g], 0.0).astype(BF16)
            gs = slice(g * GMLP_GROUP_DIM, (g + 1) * GMLP_GROUP_DIM)
            for c in range(tm // CHUNK):
                rs = slice(c * CHUNK, (c + 1) * CHUNK)
                sp = jnp.dot(w, vr16[rs, gs], preferred_element_type=F32) + bsp_ref[g]
                sg_scr[rs, gs] = (u[rs, gs] * sp).astype(BF16)
    else:
        sg_scr[...] = (u * (vr * wsp_ref[...] + bsp_ref[...])).astype(BF16)
    yb = jnp.dot(sg_scr[...], wbb_ref[...], preferred_element_type=F32)
    gyb_ref[...] = (jax.nn.sigmoid(proj(COL_GB, D_MODEL)) * yb).astype(BF16)
    sga_ref[...] = jax.nn.sigmoid(proj(COL_GA, D_MODEL)).astype(BF16)


def _inproj(x3, shift, scale, g1, win, rope_tabs, wsp, bsp, gln, bln, wbb, *, tm, chunked, emit_gv):
    B, S, D = x3.shape
    per_token = shift.shape[1] != 1
    nt = S // tm
    mod_spec = (pl.BlockSpec((None, tm, D), lambda b, i: (b, i, 0)) if per_token
                else pl.BlockSpec((None, 1, D), lambda b, i: (b, 0, 0)))
    rope_rows = rope_tabs[0].shape[0]
    rope_spec = (pl.BlockSpec((tm, LANES), lambda b, i: (i, 0)) if rope_rows != 1
                 else pl.BlockSpec((1, LANES), lambda b, i: (0, 0)))
    const2 = lambda a: pl.BlockSpec(a.shape, lambda b, i: (0,) * a.ndim, pipeline_mode=pl.Buffered(1))
    tok = lambda n: pl.BlockSpec((None, tm, n), lambda b, i: (b, i, 0))
    head_rows = pl.BlockSpec((None, tm * N_KV_HEADS, LANES), lambda b, i: (b, i, 0))
    out_shapes = [jax.ShapeDtypeStruct((B, S, Q_COLS), BF16),
                  jax.ShapeDtypeStruct((B, S * N_KV_HEADS, LANES), F32),
                  jax.ShapeDtypeStruct((B, S * N_KV_HEADS, V_DIM), F32),
                  jax.ShapeDtypeStruct((B, S, K_COLS), BF16),
                  jax.ShapeDtypeStruct((B, V_COLS, S), BF16),
                  jax.ShapeDtypeStruct((B, S, D), BF16),
                  jax.ShapeDtypeStruct((B, S, D), BF16)]
    out_specs = [tok(Q_COLS), head_rows, head_rows, tok(K_COLS),
                 pl.BlockSpec((None, V_COLS, tm), lambda b, i: (b, 0, i)), tok(D), tok(D)]
    if emit_gv:
        out_shapes.append(jax.ShapeDtypeStruct((B, S, D), F32))
        out_specs.append(tok(D))
    return pl.pallas_call(
        functools.partial(_inproj_kernel, chunked=chunked, emit_gv=emit_gv),
        grid=(B, nt),
        in_specs=[tok(D), mod_spec, mod_spec, const2(g1), const2(win),
                  rope_spec, rope_spec, rope_spec, const2(wsp), const2(bsp),
                  const2(gln), const2(bln), const2(wbb)],
        out_specs=out_specs,
        out_shape=out_shapes,
        scratch_shapes=[pltpu.VMEM((tm, D), BF16)],
        compiler_params=_params(("arbitrary", "arbitrary")),
        name="inproj",
    )(x3, shift, scale, g1, win, *rope_tabs, wsp, bsp, gln, bln, wbb)


ATTN_COL_SPLIT = 2


def _attn_prompt_kernel(lam_ref, q_ref, k_ref, vt_ref, gsub_ref, o_ref, m_scr, l_scr, acc_scr,
                        *, tq, out_scale):
    i = pl.program_id(2)
    q = q_ref[...]
    lane = lax.broadcasted_iota(I32, (tq, LANES), 1)
    lo = lane < HEAD_DIM
    zero = jnp.zeros((tq, LANES), BF16)
    heads = [q[:, g * LANES:(g + 1) * LANES] for g in range(GQA)]
    pw = tq // ATTN_COL_SPLIT
    masked_heads = [jnp.where(lo, h, zero) for h in heads] + [jnp.where(lo, zero, h) for h in heads]
    qp = [h[c * pw:(c + 1) * pw] for h in masked_heads for c in range(ATTN_COL_SPLIT)]
    n_pieces = len(qp)
    m_scr[...] = jnp.full(m_scr.shape, NEG_INF, F32)
    l_scr[...] = jnp.zeros(l_scr.shape, F32)
    acc_scr[...] = jnp.zeros(acc_scr.shape, F32)

    def step(j, masked):
        start = pl.multiple_of(j * tq, tq)
        k = k_ref[pl.ds(start, tq), :]
        vt = vt_ref[:, pl.ds(start, tq)]
        scores = [lax.dot_general(k, qp[p], (((1,), (1,)), ((), ())), preferred_element_type=F32)
                  for p in range(n_pieces)]
        for p in range(n_pieces):
            s = scores[p]
            if masked:
                kpos = lax.broadcasted_iota(I32, (tq, pw), 0)
                qpos = lax.broadcasted_iota(I32, (tq, pw), 1) + (p % ATTN_COL_SPLIT) * pw
                s = jnp.where(qpos >= kpos, s, NEG_INF)
            m_prev = m_scr[p]
            m_new = jnp.maximum(m_prev, jnp.max(s, axis=0, keepdims=True))
            alpha = jnp.exp2(m_prev - m_new)
            e = jnp.exp2(s - m_new)
            l_scr[p] = alpha * l_scr[p] + jnp.sum(e, axis=0, keepdims=True)
            acc_scr[p] = alpha * acc_scr[p] + jnp.dot(vt, e.astype(BF16), preferred_element_type=F32)
            m_scr[p] = m_new

    def body(j, carry):
        step(j, False)
        return carry

    lax.fori_loop(0, i, body, 0)
    step(i, True)

    lam = lam_ref[0, 0]
    for g in range(GQA):
        for c in range(ATTN_COL_SPLIT):
            p1 = g * ATTN_COL_SPLIT + c
            p2 = (GQA + g) * ATTN_COL_SPLIT + c
            ot = acc_scr[p1] / l_scr[p1] - lam * (acc_scr[p2] / l_scr[p2])
            ot = ot * lax.rsqrt(jnp.mean(ot * ot, axis=0, keepdims=True) + EPS) * gsub_ref[...] * out_scale
            o_ref[c * pw:(c + 1) * pw, g * V_DIM:(g + 1) * V_DIM] = ot.T.astype(o_ref.dtype)


def _attn_prompt(lam, q, kb, vt, gsub_col, *, tq, out_scale):
    B, S, _ = q.shape
    n_pieces = 2 * GQA * ATTN_COL_SPLIT
    pw = tq // ATTN_COL_SPLIT
    return pl.pallas_call(
        functools.partial(_attn_prompt_kernel, tq=tq, out_scale=out_scale),
        grid=(B, N_KV_HEADS, S // tq),
        in_specs=[pl.BlockSpec(memory_space=pltpu.SMEM),
                  pl.BlockSpec((None, tq, GQA * LANES), lambda b, n, i: (b, i, n)),
                  pl.BlockSpec((None, S, LANES), lambda b, n, i: (b, 0, n)),
                  pl.BlockSpec((None, V_DIM, S), lambda b, n, i: (b, n, 0)),
                  pl.BlockSpec((V_DIM, 1), lambda b, n, i: (0, 0))],
        out_specs=pl.BlockSpec((None, tq, GQA * V_DIM), lambda b, n, i: (b, i, n)),
        out_shape=jax.ShapeDtypeStruct((B, S, N_HEADS * V_DIM), BF16),
        scratch_shapes=[pltpu.VMEM((n_pieces, 1, pw), F32), pltpu.VMEM((n_pieces, 1, pw), F32),
                        pltpu.VMEM((n_pieces, V_DIM, pw), F32)],
        compiler_params=_params(("arbitrary", "arbitrary", "arbitrary")),
        name="attn_prompt",
    )(lam, q, kb, vt, gsub_col)


N_QROWS = 2 * N_HEADS
PAGE_ROWS = PAGE_SIZE * N_KV_HEADS
SAMPLE_LOOKAHEAD = 2


def _attn_sample_kernel(pt_ref, lam_ref, q_ref, kn_ref, vn_ref, gsub_ref, ck_ref, cv_ref, o_ref,
                        kbuf, vbuf, sems, m_scr, l_scr, acc_scr, *, pages, out_scale):
    n_slots = SAMPLE_LOOKAHEAD + 1
    b = pl.program_id(0)
    c = pl.program_id(1)
    n_c = pl.num_programs(1)
    step = b * n_c + c
    n_steps = pl.num_programs(0) * n_c
    slot = lax.rem(step, n_slots)

    def page_copies(s):
        bb = lax.div(s, n_c)
        cc = lax.rem(s, n_c)
        sl = lax.rem(s, n_slots)
        out = []
        for p in range(pages):
            page = pt_ref[bb, cc * pages + p]
            out.append(pltpu.make_async_copy(ck_ref.at[page], kbuf.at[sl, p], sems.at[sl]))
            out.append(pltpu.make_async_copy(cv_ref.at[page], vbuf.at[sl, p], sems.at[sl]))
        return out

    def start_all(copies):
        for n, cp in enumerate(copies):
            cp.start(priority=n % 2)

    for ahead in range(SAMPLE_LOOKAHEAD):
        @pl.when((step == 0) & (ahead < n_steps))
        def _():
            start_all(page_copies(jnp.int32(ahead)))

    @pl.when(step + SAMPLE_LOOKAHEAD < n_steps)
    def _():
        start_all(page_copies(step + SAMPLE_LOOKAHEAD))

    for cp in page_copies(step):
        cp.wait()

    @pl.when(c == 0)
    def _():
        m_scr[...] = jnp.full(m_scr.shape, NEG_INF, F32)
        l_scr[...] = jnp.zeros(l_scr.shape, F32)
        acc_scr[...] = jnp.zeros(acc_scr.shape, F32)

    q = q_ref[...]
    r = lax.broadcasted_iota(I32, (N_QROWS, PAGE_ROWS), 0)
    col = lax.broadcasted_iota(I32, (N_QROWS, PAGE_ROWS), 1)
    valid = (col & (N_KV_HEADS - 1)) == ((r & (N_HEADS - 1)) >> 1)
    k_pages = [kbuf[slot, p] for p in range(pages)]
    v_pages = [vbuf[slot, p] for p in range(pages)]
    scores = []
    for p in range(pages):
        s = lax.dot_general(q, k_pages[p].astype(BF16), (((1,), (1,)), ((), ())),
                            preferred_element_type=F32)
        scores.append(jnp.where(valid, s, NEG_INF))
    m_prev = m_scr[...]
    m_new = m_prev
    for s in scores:
        m_new = jnp.maximum(m_new, jnp.max(s, axis=-1, keepdims=True))
    alpha = jnp.exp2(m_prev - m_new)
    l_new = alpha * l_scr[...]
    acc = alpha * acc_scr[...]
    for p in range(pages):
        e = jnp.where(valid, jnp.exp2(scores[p] - m_new), 0.0)
        l_new = l_new + jnp.sum(e, axis=-1, keepdims=True)
        acc = acc + jnp.dot(e.astype(BF16), v_pages[p].astype(BF16), preferred_element_type=F32)
    m_scr[...] = m_new
    l_scr[...] = l_new
    acc_scr[...] = acc

    @pl.when(c == pl.num_programs(1) - 1)
    def _():
        s_self = jnp.sum(q.astype(F32) * kn_ref[...], axis=-1, keepdims=True)
        m_fin = jnp.maximum(m_new, s_self)
        a = jnp.exp2(m_new - m_fin)
        e_self = jnp.exp2(s_self - m_fin)
        o = (a * acc + e_self * vn_ref[...]) / (a * l_new + e_self)
        lam = lam_ref[0, 0]
        od = o[:N_HEADS] - lam * o[N_HEADS:]
        o_ref[...] = (_rmsnorm(od, gsub_ref[...]) * out_scale).astype(o_ref.dtype)


def _attn_sample(page_table, lam, q16, kn16, vn16, gsub, ck3, cv3, *, pages, out_scale):
    DB, n_pages = page_table.shape
    steps = n_pages // pages
    row3 = lambda n: pl.BlockSpec((None, n, LANES), lambda b, c, pt: (b, 0, 0))
    grid_spec = pltpu.PrefetchScalarGridSpec(
        num_scalar_prefetch=1,
        grid=(DB, steps),
        in_specs=[pl.BlockSpec(memory_space=pltpu.SMEM), row3(N_QROWS), row3(N_QROWS), row3(N_QROWS),
                  pl.BlockSpec((1, V_DIM), lambda b, c, pt: (0, 0)),
                  pl.BlockSpec(memory_space=pl.ANY), pl.BlockSpec(memory_space=pl.ANY)],
        out_specs=row3(N_HEADS),
        scratch_shapes=[pltpu.VMEM((SAMPLE_LOOKAHEAD + 1, pages, PAGE_ROWS, LANES), F32),
                        pltpu.VMEM((SAMPLE_LOOKAHEAD + 1, pages, PAGE_ROWS, LANES), F32),
                        pltpu.SemaphoreType.DMA((SAMPLE_LOOKAHEAD + 1,)),
                        pltpu.VMEM((N_QROWS, 1), F32), pltpu.VMEM((N_QROWS, 1), F32),
                        pltpu.VMEM((N_QROWS, V_DIM), F32)],
    )
    return pl.pallas_call(
        functools.partial(_attn_sample_kernel, pages=pages, out_scale=out_scale),
        grid_spec=grid_spec,
        out_shape=jax.ShapeDtypeStruct((DB, N_HEADS, V_DIM), BF16),
        compiler_params=_params(("arbitrary", "arbitrary")),
        name="attn_sample",
    )(page_table, lam, q16, kn16, vn16, gsub, ck3, cv3)


def _post_kernel(o_ref, sga_ref, gyb_ref, x_ref, gate_ref, shift_ref, scale_ref, g2_ref,
                 wba_ref, wout_ref, wr_ref, br_ref,
                 x1_ref, h2p_ref, route_ref, counts_ref, base_scr):
    tm = x_ref.shape[0]
    first = (pl.program_id(0) == 0) & (pl.program_id(1) == 0)

    @pl.when(first)
    def _():
        base_scr[...] = jnp.zeros(base_scr.shape, F32)

    ya = jnp.dot(o_ref[...], wba_ref[...], preferred_element_type=F32)
    merged = sga_ref[...].astype(F32) * ya + gyb_ref[...].astype(F32)
    upd = jnp.dot(merged.astype(BF16), wout_ref[...], preferred_element_type=F32)
    x1 = x_ref[...] + gate_ref[...] * upd
    x1_ref[...] = x1
    h2 = _rmsnorm(x1, g2_ref[...]) * (1.0 + scale_ref[...]) + shift_ref[...]
    half = D_MODEL // 2
    h2p_ref[...] = _pack_pair(h2[:, :half], h2[:, half:])

    h_hi = h2.astype(BF16)
    h_lo = (h2 - h_hi.astype(F32)).astype(BF16)
    logits = (jnp.dot(h_hi, wr_ref[0], preferred_element_type=F32)
              + (jnp.dot(h_hi, wr_ref[1], preferred_element_type=F32)
                 + jnp.dot(h_lo, wr_ref[0], preferred_element_type=F32))) + br_ref[...]
    lane = lax.broadcasted_iota(I32, (tm, LANES), 1)
    lanef = lane.astype(F32)
    big = float(LANES)
    is_g = lane < N_GROUPS
    lg = jnp.where(is_g, logits, -jnp.inf)
    gmax = jnp.max(lg, axis=-1, keepdims=True)
    g_idx = jnp.min(jnp.where(lg == gmax, lanef, big), axis=-1, keepdims=True)
    p_grp = 1.0 / jnp.sum(jnp.where(is_g, jnp.exp(logits - gmax), 0.0), axis=-1, keepdims=True)
    e_lo = N_GROUPS + EXPERTS_PER_GROUP * g_idx
    in_grp = (lanef >= e_lo) & (lanef < e_lo + EXPERTS_PER_GROUP)
    emax = jnp.max(jnp.where(in_grp, logits, -jnp.inf), axis=-1, keepdims=True)
    ee = jnp.where(in_grp, jnp.exp(logits - emax), 0.0)
    pe = ee / jnp.sum(ee, axis=-1, keepdims=True)
    pe_m = jnp.where(in_grp, pe, -1.0)
    v1 = jnp.max(pe_m, axis=-1, keepdims=True)
    i1 = jnp.min(jnp.where(pe_m == v1, lanef, big), axis=-1, keepdims=True)
    pe_m2 = jnp.where(lanef == i1, -1.0, pe_m)
    v2 = jnp.max(pe_m2, axis=-1, keepdims=True)
    i2 = jnp.min(jnp.where(pe_m2 == v2, lanef, big), axis=-1, keepdims=True)
    tsum = v1 + v2
    w1 = p_grp * (v1 / tsum)
    w2 = p_grp * (v2 / tsum)
    e1 = i1 - N_GROUPS
    e2 = i2 - N_GROUPS

    oh1 = (lanef == e1).astype(F32)
    oh2 = (lanef == e2).astype(F32)
    ohs = oh1 + oh2
    rr = lax.broadcasted_iota(I32, (tm, tm), 0)
    cc = lax.broadcasted_iota(I32, (tm, tm), 1)
    tri = jnp.where(rr > cc, 1.0, 0.0).astype(BF16)
    before = jnp.dot(tri, ohs.astype(BF16), preferred_element_type=F32) + base_scr[...]
    r1 = jnp.sum(before * oh1, axis=-1, keepdims=True)
    r2 = jnp.sum(before * oh2, axis=-1, keepdims=True)
    new_base = base_scr[...] + jnp.sum(ohs, axis=0, keepdims=True)
    base_scr[...] = new_base
    counts_ref[...] = jnp.broadcast_to(new_base, counts_ref.shape)

    route = jnp.zeros((tm, LANES), F32)
    for idx, val in enumerate((e1, e2, r1, r2, w1, w2)):
        route = jnp.where(lane == idx, val, route)
    route_ref[...] = route


def _post(o, sga, gyb, x3, gate, shift, scale, g2, wba, wout, wr, br, *, tm):
    B, S, D = x3.shape
    per_token = gate.shape[1] != 1
    mod_spec = (pl.BlockSpec((None, tm, D), lambda b, i: (b, i, 0)) if per_token
                else pl.BlockSpec((None, 1, D), lambda b, i: (b, 0, 0)))
    const2 = lambda a: pl.BlockSpec(a.shape, lambda b, i: (0,) * a.ndim, pipeline_mode=pl.Buffered(1))
    tok = lambda n: pl.BlockSpec((None, tm, n), lambda b, i: (b, i, 0))
    return pl.pallas_call(
        _post_kernel,
        grid=(B, S // tm),
        in_specs=[tok(D), tok(D), tok(D), tok(D), mod_spec, mod_spec, mod_spec, const2(g2),
                  const2(wba), const2(wout), const2(wr), const2(br)],
        out_specs=[tok(D), tok(D // 2), tok(LANES), pl.BlockSpec((8, LANES), lambda b, i: (0, 0))],
        out_shape=[jax.ShapeDtypeStruct((B, S, D), F32),
                   jax.ShapeDtypeStruct((B, S, D // 2), I32),
                   jax.ShapeDtypeStruct((B, S, LANES), F32),
                   jax.ShapeDtypeStruct((8, LANES), F32)],
        scratch_shapes=[pltpu.VMEM((1, LANES), F32)],
        compiler_params=_params(("arbitrary", "arbitrary")),
        name="post",
    )(o, sga, gyb, x3, gate, shift, scale, g2, wba, wout, wr, br)


ROW_UNROLL = 8


def _row_copy(src, dst, src_row, dst_row, sem):
    return pltpu.make_async_copy(src.at[pl.ds(src_row, 1)], dst.at[pl.ds(dst_row, 1)], sem)


def _dispatch_kernel(dest_ref, h_ref, xs_in_ref, xs_ref, dest_smem, sem, dsem):
    del xs_in_ref
    tm = h_ref.shape[0]
    cp = pltpu.make_async_copy(dest_ref.at[0], dest_smem, dsem)
    cp.start()
    cp.wait()

    def issue(blk, carry):
        for u in range(ROW_UNROLL):
            r = blk * ROW_UNROLL + u
            _row_copy(h_ref, xs_ref, r, dest_smem[0, 2 * r], sem).start(priority=0)
            _row_copy(h_ref, xs_ref, r, dest_smem[0, 2 * r + 1], sem).start(priority=1)
        return carry

    lax.fori_loop(0, tm // ROW_UNROLL, issue, 0)
    for _ in range(2):
        pltpu.make_async_copy(h_ref, xs_ref.at[pl.ds(0, tm)], sem).wait()


def _dispatch(dest, h2p, slots, *, tm):
    T, W = h2p.shape
    nt = T // tm
    xs0 = jnp.zeros((slots, W), I32)
    return pl.pallas_call(
        _dispatch_kernel,
        grid=(nt,),
        in_specs=[pl.BlockSpec((1, 1, 2 * tm), lambda i: (i, 0, 0)),
                  pl.BlockSpec((tm, W), lambda i: (i, 0)),
                  pl.BlockSpec(memory_space=pl.ANY)],
        out_specs=pl.BlockSpec(memory_space=pl.ANY),
        out_shape=jax.ShapeDtypeStruct((slots, W), I32),
        scratch_shapes=[pltpu.SMEM((1, 2 * tm), I32), pltpu.SemaphoreType.DMA(()), pltpu.SemaphoreType.DMA(())],
        input_output_aliases={2: 0},
        compiler_params=_params(("arbitrary",)),
        name="dispatch",
    )(dest.reshape(nt, 1, 2 * tm), h2p, xs0)


SC_GATHER_ROWS = 64


def _sc_gather(table, idx):
    info = plsc.get_sparse_core_info()
    n_workers = info.num_cores * info.num_subcores
    B = idx.shape[0]
    W = table.shape[1]
    R = SC_GATHER_ROWS
    per_worker = B // n_workers
    n_chunks = per_worker // R
    assert per_worker * n_workers == B and n_chunks * R == per_worker and n_chunks % 2 == 0
    mesh = plsc.VectorSubcoreMesh(core_axis_name="c", subcore_axis_name="s")

    def body(table_hbm, idx_hbm, out_hbm, idx_v, rows_v, sems):
        wid = lax.axis_index("s") * info.num_cores + lax.axis_index("c")
        base = wid * per_worker

        def gather(chunk, slot):
            off = pl.multiple_of(base + chunk * R, 8)
            pltpu.sync_copy(idx_hbm.at[pl.ds(off, R)], idx_v.at[slot])
            return pltpu.make_async_copy(table_hbm.at[idx_v.at[slot]], rows_v.at[slot], sems.at[slot])

        def flush(chunk, slot):
            off = pl.multiple_of(base + chunk * R, 8)
            pltpu.sync_copy(rows_v.at[slot], out_hbm.at[pl.ds(off, R)])

        gather(0, 0).start()

        @pl.loop(0, n_chunks, step=2)
        def _(g):
            gather(g + 1, 1).start()
            pltpu.make_async_copy(table_hbm.at[idx_v.at[0]], rows_v.at[0], sems.at[0]).wait()
            flush(g, 0)

            @pl.when(g + 2 < n_chunks)
            def _():
                gather(g + 2, 0).start()

            pltpu.make_async_copy(table_hbm.at[idx_v.at[1]], rows_v.at[1], sems.at[1]).wait()
            flush(g + 1, 1)

    return pl.kernel(
        body,
        out_type=jax.ShapeDtypeStruct((B, W), I32),
        mesh=mesh,
        scratch_types=[pltpu.VMEM((2, R), I32), pltpu.VMEM((2, R, W), I32), pltpu.SemaphoreType.DMA((2,))],
        name="sc_gather",
    )(table, idx)


def _ffn_kernel(be_ref, nu_ref, x_ref, wg_ref, wu_ref, wd_ref, y_ref):
    del be_ref
    i = pl.program_id(0)

    @pl.when(i < nu_ref[0])
    def _():
        a, b = _unpack_pair(x_ref[...])
        x = jnp.concatenate([a.astype(BF16), b.astype(BF16)], axis=1)
        hg = jnp.dot(x, wg_ref[...].astype(BF16), preferred_element_type=F32)
        hu = jnp.dot(x, wu_ref[...].astype(BF16), preferred_element_type=F32)
        hid = (hg * jax.nn.sigmoid(hg)) * hu
        y = jnp.dot(hid.astype(BF16), wd_ref[...].astype(BF16), preferred_element_type=F32)
        half = D_MODEL // 2
        y_ref[...] = _pack_pair(y[:, :half], y[:, half:])

    @pl.when(i >= nu_ref[0])
    def _():
        y_ref[...] = jnp.zeros(y_ref.shape, I32)


def _ffn(block_expert, n_used, xs, weg, weu, wed, *, bm):
    slots, W = xs.shape
    nb = slots // bm
    grid_spec = pltpu.PrefetchScalarGridSpec(
        num_scalar_prefetch=2,
        grid=(nb,),
        in_specs=[pl.BlockSpec((bm, W), lambda i, be, nu: (i, 0)),
                  pl.BlockSpec((None, D_MODEL, D_EXPERT), lambda i, be, nu: (be[i], 0, 0)),
                  pl.BlockSpec((None, D_MODEL, D_EXPERT), lambda i, be, nu: (be[i], 0, 0)),
                  pl.BlockSpec((None, D_EXPERT, D_MODEL), lambda i, be, nu: (be[i], 0, 0))],
        out_specs=pl.BlockSpec((bm, W), lambda i, be, nu: (i, 0)),
    )
    return pl.pallas_call(
        _ffn_kernel,
        grid_spec=grid_spec,
        out_shape=jax.ShapeDtypeStruct((slots, W), I32),
        compiler_params=_params(("arbitrary",)),
        name="ffn",
    )(block_expert, n_used, xs, weg, weu, wed)


def _combine_kernel(dest_ref, dest_next_ref, ys_ref, x1_ref, route_ref, gate_ref, fshift_ref, fscale_ref,
                    gf_ref, o_ref, dest_smem, rows_scr, sems, dsem):
    tm = x1_ref.shape[0]
    step = pl.program_id(0) * pl.num_programs(1) + pl.program_id(1)
    n_steps = pl.num_programs(0) * pl.num_programs(1)
    slot = lax.rem(step, 2)

    def gather(dref, sl):
        cp = pltpu.make_async_copy(dref.at[0], dest_smem, dsem)
        cp.start()
        cp.wait()

        def issue(blk, carry):
            for u in range(ROW_UNROLL):
                r = blk * ROW_UNROLL + u
                _row_copy(ys_ref, rows_scr.at[sl, 0], dest_smem[0, 2 * r], r, sems.at[sl]).start(priority=0)
                _row_copy(ys_ref, rows_scr.at[sl, 1], dest_smem[0, 2 * r + 1], r, sems.at[sl]).start(priority=1)
            return carry

        lax.fori_loop(0, tm // ROW_UNROLL, issue, 0)

    @pl.when(step == 0)
    def _():
        gather(dest_ref, 0)

    @pl.when(step + 1 < n_steps)
    def _():
        gather(dest_next_ref, 1 - slot)

    for half in range(2):
        pltpu.make_async_copy(ys_ref.at[pl.ds(0, tm)], rows_scr.at[slot, half], sems.at[slot]).wait()

    o_ref[...] = _combine_math(rows_scr[slot, 0], rows_scr[slot, 1], x1_ref[...], route_ref[...],
                               gate_ref[...], fshift_ref[...], fscale_ref[...], gf_ref[...])


def _combine_math(rows1, rows2, x1, route, gate, fshift, fscale, gf):
    w1 = route[:, 4:5]
    w2 = route[:, 5:6]
    a1, b1 = _unpack_pair(rows1)
    a2, b2 = _unpack_pair(rows2)
    y = jnp.concatenate([a1 * w1 + a2 * w2, b1 * w1 + b2 * w2], axis=1)
    x2 = x1 + gate * y
    return _rmsnorm(x2, gf) * (1.0 + fscale) + fshift


def _combine_dense_kernel(rows_ref, x1_ref, route_ref, gate_ref, fshift_ref, fscale_ref, gf_ref, o_ref):
    o_ref[...] = _combine_math(rows_ref[0], rows_ref[1], x1_ref[...], route_ref[...],
                               gate_ref[...], fshift_ref[...], fscale_ref[...], gf_ref[...])


def _combine_dense(rows, x1, route, gate, fshift, fscale, gf, *, tm):
    B, S, D = x1.shape
    W = rows.shape[-1]
    per_token = gate.shape[1] != 1
    mod_spec = (pl.BlockSpec((None, tm, D), lambda b, i: (b, i, 0)) if per_token
                else pl.BlockSpec((None, 1, D), lambda b, i: (b, 0, 0)))
    tok = lambda n: pl.BlockSpec((None, tm, n), lambda b, i: (b, i, 0))
    return pl.pallas_call(
        _combine_dense_kernel,
        grid=(B, S // tm),
        in_specs=[pl.BlockSpec((2, None, tm, W), lambda b, i: (0, b, i, 0)),
                  tok(D), tok(LANES), mod_spec, mod_spec, mod_spec,
                  pl.BlockSpec(gf.shape, lambda b, i: (0, 0))],
        out_specs=tok(D),
        out_shape=jax.ShapeDtypeStruct((B, S, D), F32),
        compiler_params=_params(("arbitrary", "arbitrary")),
        name="combine_dense",
    )(rows, x1, route, gate, fshift, fscale, gf)


def _combine(dest, ys, x1, route, gate, fshift, fscale, gf, *, tm):
    B, S, D = x1.shape
    nt = S // tm
    W = ys.shape[1]
    per_token = gate.shape[1] != 1
    mod_spec = (pl.BlockSpec((None, tm, D), lambda b, i: (b, i, 0)) if per_token
                else pl.BlockSpec((None, 1, D), lambda b, i: (b, 0, 0)))
    tok = lambda n: pl.BlockSpec((None, tm, n), lambda b, i: (b, i, 0))
    dest3 = dest.reshape(B * nt, 1, 2 * tm)
    return pl.pallas_call(
        _combine_kernel,
        grid=(B, nt),
        in_specs=[pl.BlockSpec((1, 1, 2 * tm), lambda b, i: (b * nt + i, 0, 0)),
                  pl.BlockSpec((1, 1, 2 * tm), lambda b, i: (jnp.minimum(b * nt + i + 1, B * nt - 1), 0, 0)),
                  pl.BlockSpec(memory_space=pl.ANY),
                  tok(D), tok(LANES), mod_spec, mod_spec, mod_spec,
                  pl.BlockSpec(gf.shape, lambda b, i: (0, 0))],
        out_specs=tok(D),
        out_shape=jax.ShapeDtypeStruct((B, S, D), F32),
        scratch_shapes=[pltpu.SMEM((1, 2 * tm), I32), pltpu.VMEM((2, 2, tm, W), I32),
                        pltpu.SemaphoreType.DMA((2,)), pltpu.SemaphoreType.DMA(())],
        compiler_params=_params(("arbitrary", "arbitrary")),
        name="combine",
    )(dest3, dest3, ys, x1, route, gate, fshift, fscale, gf)


def _rope_tables(pos):
    half = ROPE_DIM // 2
    inv_freq = ROPE_THETA ** (-jnp.arange(0, ROPE_DIM, 2, dtype=F32) / ROPE_DIM)
    ang = pos.astype(F32)[:, None] * inv_freq[None, :]
    cos, sin = jnp.cos(ang), jnp.sin(ang)
    n = pos.shape[0]
    rest = HEAD_DIM - ROPE_DIM
    c64 = jnp.concatenate([cos, cos, jnp.ones((n, rest), F32)], axis=1)
    a64 = jnp.concatenate([-sin, jnp.zeros((n, HEAD_DIM - half), F32)], axis=1)
    b64 = jnp.concatenate([jnp.zeros((n, half), F32), sin, jnp.zeros((n, rest), F32)], axis=1)
    return tuple(jnp.tile(t, (1, LANES // HEAD_DIM)) for t in (c64, a64, b64))


def _moe_plan(route, counts, bm):
    T = route.shape[0]
    e = route[:, 0:2].astype(I32)
    rank = route[:, 2:4].astype(I32)
    cnt = counts[0, :N_EXPERTS].astype(I32)
    padded = (cnt + bm - 1) // bm * bm
    pad_end = jnp.cumsum(padded)
    pad_start = pad_end - padded
    ids = jnp.arange(N_EXPERTS, dtype=I32)
    dest = rank + jnp.sum(jnp.where(e[..., None] == ids, pad_start, 0), axis=-1)
    n_blocks = -(-(2 * T) // bm) + N_EXPERTS
    block_lo = jnp.arange(n_blocks, dtype=I32) * bm
    block_expert = jnp.minimum(jnp.sum((pad_end[None, :] <= block_lo[:, None]).astype(I32), axis=1),
                               N_EXPERTS - 1)
    n_used = (pad_end[-1:] // bm).astype(I32)
    return dest.reshape(-1), block_expert, n_used, n_blocks * bm


def _slot_tokens(route, counts, bm, slots):
    T = route.shape[0]
    e_flat = route[:, 0:2].astype(I32).reshape(-1)
    cnt = counts[0, :N_EXPERTS].astype(I32)
    padded = (cnt + bm - 1) // bm * bm
    pad_end = jnp.cumsum(padded)
    pad_start = pad_end - padded
    start = jnp.cumsum(cnt) - cnt
    order = jnp.argsort(e_flat, stable=True).astype(I32)
    s = jnp.arange(slots, dtype=I32)
    eb = jnp.minimum(jnp.sum((pad_end[None, :] <= s[:, None]).astype(I32), axis=1), N_EXPERTS - 1)
    hot = eb[:, None] == jnp.arange(N_EXPERTS, dtype=I32)
    pick = lambda tab: jnp.sum(jnp.where(hot, tab, 0), axis=1)
    r = s - pick(pad_start)
    valid = r < pick(cnt)
    src = jnp.clip(pick(start) + r, 0, 2 * T - 1)
    return jnp.where(valid, order[src] // 2, s % T)


def _layer(x3, mod, modf, pos_tabs, attend, lam_init, wts, *, tm, tmr, bm, chunked, emit_gv, sc_rows):
    B, S, D = x3.shape
    m = lambda k: mod[:, :, k, :]
    outs = _inproj(x3, m(0), m(1), wts['g1'], wts['win'], pos_tabs, wts['wsp'], wts['bsp'],
                   wts['gln'], wts['bln'], wts['wbb'], tm=tm, chunked=chunked, emit_gv=emit_gv)
    q, kf, vf, kb, vt, sga, gyb = outs[:7]
    gv = outs[7] if emit_gv else None
    o = attend(q, kf, vf, kb, vt)
    x1, h2p, route, counts = _post(o, sga, gyb, x3, m(2), m(3), m(4), wts['g2'], wts['wba'],
                                   wts['wout'], wts['wr'], wts['br'], tm=tm)
    T = B * S
    W = D // 2
    dest, block_expert, n_used, slots = _moe_plan(route.reshape(T, LANES), counts, bm)
    fin = (m(5), modf[:, :, 0, :], modf[:, :, 1, :], wts['gf'])
    if sc_rows:
        xs = _sc_gather(h2p.reshape(T, W), _slot_tokens(route.reshape(T, LANES), counts, bm, slots))
        ys = _ffn(block_expert, n_used, xs, wts['weg'], wts['weu'], wts['wed'], bm=bm)
        rows = _sc_gather(ys, dest.reshape(T, 2).T.reshape(-1)).reshape(2, B, S, W)
        y = _combine_dense(rows, x1, route, *fin, tm=tm)
    else:
        xs = _dispatch(dest, h2p.reshape(T, W), slots, tm=tmr)
        ys = _ffn(block_expert, n_used, xs, wts['weg'], wts['weu'], wts['wed'], bm=bm)
        y = _combine(dest, ys, x1, route, *fin, tm=tmr)
    return y, kf, vf, gv


def kernel(x_prompt, x_sample, cache_k, cache_v, page_table, c_prompt, c_sample, w_ada, b_ada, w_ada_final, b_ada_final, g_norm1, g_norm2, g_final, w_in, lambda_q1, lambda_k1, lambda_q2, lambda_k2, g_subln, g_gmlp_ln, b_gmlp_ln, w_spatial, b_spatial, w_branch_a, w_branch_b, w_out, w_router_group, b_router_group, w_router_expert, b_router_expert, w_exp_gate, w_exp_up, w_exp_down):
    depth = w_ada.shape[0]
    assert depth == 1
    B, S, D = x_prompt.shape
    DB, DS, _ = x_sample.shape
    assert DS == 1 and D == D_MODEL
    n_pages = page_table.shape[1]
    n_past = n_pages * PAGE_SIZE
    l = 0
    lam_init = 0.8 - 0.6 * math.exp(-0.3 * l)
    out_scale = 1.0 - lam_init

    c_all = jnp.concatenate([c_prompt, c_sample], axis=0)
    mod_all = _adaln(c_all, w_ada[l], b_ada[l])
    modf_all = _adaln(c_all, w_ada_final, b_ada_final)
    mod_p = mod_all[:B].reshape(B, 1, 6, D)
    mod_s = mod_all[B:].reshape(1, DB, 6, D)
    modf_p = modf_all[:B].reshape(B, 1, 2, D)
    modf_s = modf_all[B:].reshape(1, DB, 2, D)
    lam = _lam(lambda_q1[l], lambda_k1[l], lambda_q2[l], lambda_k2[l], lam_init)[0:1, 0:1]

    row = lambda a: a.reshape(1, -1)
    wr = jnp.zeros((D, LANES), F32)
    wr = wr.at[:, :N_GROUPS].set(w_router_group[l]).at[:, N_GROUPS:N_GROUPS + N_EXPERTS].set(w_router_expert[l])
    wr_hi = wr.astype(BF16)
    wr_lo = (wr - wr_hi.astype(F32)).astype(BF16)
    br = jnp.zeros((1, LANES), F32)
    br = br.at[0, :N_GROUPS].set(b_router_group[l]).at[0, N_GROUPS:N_GROUPS + N_EXPERTS].set(b_router_expert[l])
    wts = dict(
        g1=row(g_norm1[l]), g2=row(g_norm2[l]), gf=row(g_final), win=w_in[l].astype(BF16),
        gln=row(g_gmlp_ln[l]), bln=row(b_gmlp_ln[l]), wbb=w_branch_b[l].astype(BF16),
        wba=w_branch_a[l].astype(BF16), wout=w_out[l].astype(BF16), wr=jnp.stack([wr_hi, wr_lo]), br=br,
        weg=w_exp_gate[l], weu=w_exp_up[l], wed=w_exp_down[l])
    gsub = row(g_subln[l])

    wts_s = dict(wts, wsp=row(jnp.repeat(w_spatial[l][:, 0, 0], GMLP_GROUP_DIM)),
                 bsp=row(jnp.repeat(b_spatial[l][:, 0], GMLP_GROUP_DIM)))
    tabs_s = _rope_tables(jnp.full((1,), n_past, dtype=I32))
    ck3 = cache_k[l].reshape(-1, PAGE_ROWS, LANES)
    cv3 = cache_v[l].reshape(-1, PAGE_ROWS, LANES)
    half_mask = (jnp.arange(LANES)[None, :] // HEAD_DIM == jnp.arange(2)[:, None])

    def attend_s(q, kf, vf, kb, vb):
        del kb, vb
        qh = q.reshape(DB, 1, N_HEADS, LANES)
        q16 = jnp.where(half_mask[None, :, None, :], qh, jnp.zeros_like(qh)).reshape(DB, N_QROWS, LANES)
        kv_of_row = (jnp.arange(N_QROWS) % N_HEADS) // GQA
        kn16 = kf.reshape(DB, N_KV_HEADS, LANES)[:, kv_of_row]
        vn16 = vf.reshape(DB, N_KV_HEADS, V_DIM)[:, kv_of_row]
        pages = 8 if n_pages % 8 == 0 else (4 if n_pages % 4 == 0 else 1)
        o = _attn_sample(page_table, lam, q16, kn16, vn16, gsub, ck3, cv3, pages=pages, out_scale=out_scale)
        return o.reshape(1, DB, N_HEADS * V_DIM)

    y_s, k_s, v_s, gv_s = _layer(x_sample.reshape(1, DB, D), mod_s, modf_s, tabs_s, attend_s, lam_init,
                                 wts_s, tm=DB, tmr=DB, bm=128, chunked=False, emit_gv=True, sc_rows=False)

    wts_p = dict(wts, wsp=w_spatial[l],
                 bsp=jnp.broadcast_to(b_spatial[l][:, :, None], (GMLP_GROUPS, CHUNK, GMLP_GROUP_DIM)))
    tabs_p = _rope_tables(jnp.arange(S, dtype=I32))
    tq = min(512, S)
    gsub_col = g_subln[l].reshape(V_DIM, 1)
    attend_p = lambda q, kf, vf, kb, vt: _attn_prompt(lam, q, kb, vt, gsub_col, tq=tq, out_scale=out_scale)
    y_p, k_p, v_p, _ = _layer(x_prompt, mod_p, modf_p, tabs_p, attend_p, lam_init, wts_p,
                              tm=min(512, S), tmr=min(256, S), bm=512, chunked=True, emit_gv=False,
                              sc_rows=True)

    return (y_p,
            y_s.reshape(DB, 1, D),
            k_p.reshape(1, B, S, N_KV_HEADS, 2 * HEAD_DIM),
            v_p.reshape(1, B, S, N_KV_HEADS, V_DIM),
            k_s.reshape(1, DB, 1, N_KV_HEADS, 2 * HEAD_DIM),
            v_s.reshape(1, DB, 1, N_KV_HEADS, V_DIM),
            gv_s.reshape(1, DB, 1, D))
```

```python
import functools
import math

import jax
import jax.numpy as jnp
from jax import lax
from jax.experimental import pallas as pl
from jax.experimental.pallas import tpu as pltpu
from jax.experimental.pallas import tpu_sc as plsc

F32 = jnp.float32
BF16 = jnp.bfloat16
U32 = jnp.uint32
I32 = jnp.int32

D_MODEL = 1024
N_HEADS = 8
N_KV_HEADS = 4
GQA = N_HEADS // N_KV_HEADS
HEAD_DIM = 64
V_DIM = 128
ROPE_DIM = 16
ROPE_THETA = 500000.0
PAGE_SIZE = 128
GMLP_GROUPS = 8
GMLP_GROUP_DIM = 128
CHUNK = 128
N_GROUPS = 4
EXPERTS_PER_GROUP = 8
N_EXPERTS = 32
D_EXPERT = 512
Q_COLS = 1024
K_COLS = 512
V_COLS = 512
COL_Q, COL_K, COL_V, COL_GU, COL_GV, COL_GA, COL_GB = 0, 1024, 1536, 2048, 3072, 4096, 5120
EPS = 1e-6
NEG_INF = -1e30
LANES = 128
VMEM_LIMIT = 56 * 1024 * 1024

HIGHEST = lax.Precision.HIGHEST
Q_SCALE = HEAD_DIM ** -0.5 * math.log2(math.e)


def _params(sem, flags=None):
    return pltpu.CompilerParams(dimension_semantics=sem, vmem_limit_bytes=VMEM_LIMIT, flags=flags)


def _rmsnorm(x, g):
    return x * lax.rsqrt(jnp.mean(x * x, axis=-1, keepdims=True) + EPS) * g


def _pack_pair(a, b):
    ua = lax.bitcast_convert_type(a.astype(BF16).astype(F32), U32)
    ub = lax.bitcast_convert_type(b.astype(BF16).astype(F32), U32)
    return lax.bitcast_convert_type((ua >> 16) | (ub & jnp.uint32(0xFFFF0000)), I32)


def _unpack_pair(w):
    w = lax.bitcast_convert_type(w, U32)
    a = lax.bitcast_convert_type(w << 16, F32)
    b = lax.bitcast_convert_type(w & jnp.uint32(0xFFFF0000), F32)
    return a, b


def _adaln_kernel(c_ref, w_ref, b_ref, o_ref):
    c = c_ref[...]
    a = c * jax.nn.sigmoid(c)
    o_ref[...] = jnp.dot(a, w_ref[...], preferred_element_type=F32, precision=HIGHEST) + b_ref[...]


def _adaln(c, w, b, tn=512):
    rows, d = c.shape
    n = w.shape[1]
    return pl.pallas_call(
        _adaln_kernel,
        grid=(n // tn,),
        in_specs=[pl.BlockSpec((rows, d), lambda j: (0, 0)),
                  pl.BlockSpec((d, tn), lambda j: (0, j)),
                  pl.BlockSpec((1, tn), lambda j: (0, j))],
        out_specs=pl.BlockSpec((rows, tn), lambda j: (0, j)),
        out_shape=jax.ShapeDtypeStruct((rows, n), F32),
        compiler_params=_params(("arbitrary",)),
        name="adaln",
    )(c, w, b.reshape(1, n))


def _lam_kernel(q1_ref, k1_ref, q2_ref, k2_ref, o_ref, *, lam_init):
    a = jnp.sum(q1_ref[...] * k1_ref[...], axis=-1, keepdims=True)
    b = jnp.sum(q2_ref[...] * k2_ref[...], axis=-1, keepdims=True)
    lam = jnp.exp(a) - jnp.exp(b) + lam_init
    o_ref[...] = jnp.broadcast_to(lam, o_ref.shape)


def _lam(q1, k1, q2, k2, lam_init):
    args = [a.reshape(1, HEAD_DIM) for a in (q1, k1, q2, k2)]
    return pl.pallas_call(
        functools.partial(_lam_kernel, lam_init=lam_init),
        out_shape=jax.ShapeDtypeStruct((8, LANES), F32),
        name="lam",
    )(*args)


def _inproj_kernel(x_ref, shift_ref, scale_ref, g1_ref, win_ref, rc_ref, ra_ref, rb_ref,
                   wsp_ref, bsp_ref, gln_ref, bln_ref, wbb_ref,
                   q_ref, kf_ref, vf_ref, kb_ref, vt_ref, sga_ref, gyb_ref, *rest,
                   chunked, emit_gv):
    if emit_gv:
        gv_ref, sg_scr = rest
    else:
        (sg_scr,) = rest
    tm = x_ref.shape[0]
    x = x_ref[...]
    h = _rmsnorm(x, g1_ref[...]) * (1.0 + scale_ref[...]) + shift_ref[...]
    hb = h.astype(BF16)

    def proj(c0, n):
        return jnp.dot(hb, win_ref[:, c0:c0 + n], preferred_element_type=F32)

    rc, ra, rb = rc_ref[...], ra_ref[...], rb_ref[...]

    def rope(z):
        return z * rc + pltpu.roll(z, LANES - ROPE_DIM // 2, 1) * ra + pltpu.roll(z, ROPE_DIM // 2, 1) * rb

    zq = proj(COL_Q, Q_COLS)
    for j in range(Q_COLS // LANES):
        z = rope(zq[:, j * LANES:(j + 1) * LANES])
        q_ref[:, j * LANES:(j + 1) * LANES] = (z * Q_SCALE).astype(BF16)
    zk = proj(COL_K, K_COLS)
    for j in range(N_KV_HEADS):
        z = rope(zk[:, j * LANES:(j + 1) * LANES])
        kf_ref[pl.ds(j, tm, stride=N_KV_HEADS), :] = z
        kb_ref[:, j * LANES:(j + 1) * LANES] = z.astype(BF16)
    zv = proj(COL_V, V_COLS)
    for j in range(N_KV_HEADS):
        vf_ref[pl.ds(j, tm, stride=N_KV_HEADS), :] = zv[:, j * V_DIM:(j + 1) * V_DIM]
    vt_ref[...] = zv.T.astype(BF16)

    u = jax.nn.gelu(proj(COL_GU, D_MODEL))
    gv = jax.nn.gelu(proj(COL_GV, D_MODEL))
    xc = gv - jnp.mean(gv, axis=-1, keepdims=True)
    vr = xc * lax.rsqrt(jnp.mean(xc * xc, axis=-1, keepdims=True) + EPS) * gln_ref[...] + bln_ref[...]
    if emit_gv:
        gv_ref[...] = vr
    if chunked:
        vr16 = vr.astype(BF16)
        row = lax.broadcasted_iota(I32, (CHUNK, CHUNK), 0)
        col = lax.broadcasted_iota(I32, (CHUNK, CHUNK), 1)
        for g in range(GMLP_GROUPS):
            w = jnp.where(row >= col, wsp_ref[g], 0.0).astype(BF16)
            gs = slice(g * GMLP_GROUP_DIM, (g + 1) * GMLP_GROUP_DIM)
            for c in range(tm // CHUNK):
                rs = slice(c * CHUNK, (c + 1) * CHUNK)
                sp = jnp.dot(w, vr16[rs, gs], preferred_element_type=F32) + bsp_ref[g]
                sg_scr[rs, gs] = (u[rs, gs] * sp).astype(BF16)
    else:
        sg_scr[...] = (u * (vr * wsp_ref[...] + bsp_ref[...])).astype(BF16)
    yb = jnp.dot(sg_scr[...], wbb_ref[...], preferred_element_type=F32)
    gyb_ref[...] = (jax.nn.sigmoid(proj(COL_GB, D_MODEL)) * yb).astype(BF16)
    sga_ref[...] = jax.nn.sigmoid(proj(COL_GA, D_MODEL)).astype(BF16)


def _inproj(x3, shift, scale, g1, win, rope_tabs, wsp, bsp, gln, bln, wbb, *, tm, chunked, emit_gv):
    B, S, D = x3.shape
    per_token = shift.shape[1] != 1
    nt = S // tm
    mod_spec = (pl.BlockSpec((None, tm, D), lambda b, i: (b, i, 0)) if per_token
                else pl.BlockSpec((None, 1, D), lambda b, i: (b, 0, 0)))
    rope_rows = rope_tabs[0].shape[0]
    rope_spec = (pl.BlockSpec((tm, LANES), lambda b, i: (i, 0)) if rope_rows != 1
                 else pl.BlockSpec((1, LANES), lambda b, i: (0, 0)))
    const2 = lambda a: pl.BlockSpec(a.shape, lambda b, i: (0,) * a.ndim, pipeline_mode=pl.Buffered(1))
    tok = lambda n: pl.BlockSpec((None, tm, n), lambda b, i: (b, i, 0))
    head_rows = pl.BlockSpec((None, tm * N_KV_HEADS, LANES), lambda b, i: (b, i, 0))
    out_shapes = [jax.ShapeDtypeStruct((B, S, Q_COLS), BF16),
                  jax.ShapeDtypeStruct((B, S * N_KV_HEADS, LANES), F32),
                  jax.ShapeDtypeStruct((B, S * N_KV_HEADS, V_DIM), F32),
                  jax.ShapeDtypeStruct((B, S, K_COLS), BF16),
                  jax.ShapeDtypeStruct((B, V_COLS, S), BF16),
                  jax.ShapeDtypeStruct((B, S, D), BF16),
                  jax.ShapeDtypeStruct((B, S, D), BF16)]
    out_specs = [tok(Q_COLS), head_rows, head_rows, tok(K_COLS),
                 pl.BlockSpec((None, V_COLS, tm), lambda b, i: (b, 0, i)), tok(D), tok(D)]
    if emit_gv:
        out_shapes.append(jax.ShapeDtypeStruct((B, S, D), F32))
        out_specs.append(tok(D))
    return pl.pallas_call(
        functools.partial(_inproj_kernel, chunked=chunked, emit_gv=emit_gv),
        grid=(B, nt),
        in_specs=[tok(D), mod_spec, mod_spec, const2(g1), const2(win),
                  rope_spec, rope_spec, rope_spec, const2(wsp), const2(bsp),
                  const2(gln), const2(bln), const2(wbb)],
        out_specs=out_specs,
        out_shape=out_shapes,
        scratch_shapes=[pltpu.VMEM((tm, D), BF16)],
        compiler_params=_params(("arbitrary", "arbitrary")),
        name="inproj",
    )(x3, shift, scale, g1, win, *rope_tabs, wsp, bsp, gln, bln, wbb)


ATTN_COL_SPLIT = 2


def _attn_prompt_kernel(lam_ref, q_ref, k_ref, vt_ref, gsub_ref, o_ref, m_scr, l_scr, acc_scr,
                        s0_scr, s1_scr, *, tq, out_scale):
    i = pl.program_id(2)
    q = q_ref[...]
    lane = lax.broadcasted_iota(I32, (tq, LANES), 1)
    lo = lane < HEAD_DIM
    zero = jnp.zeros((tq, LANES), BF16)
    heads = [q[:, g * LANES:(g + 1) * LANES] for g in range(GQA)]
    pw = tq // ATTN_COL_SPLIT
    masked_heads = [jnp.where(lo, h, zero) for h in heads] + [jnp.where(lo, zero, h) for h in heads]
    qp = [h[c * pw:(c + 1) * pw] for h in masked_heads for c in range(ATTN_COL_SPLIT)]
    n_pieces = len(qp)
    m_scr[...] = jnp.full(m_scr.shape, NEG_INF, F32)
    l_scr[...] = jnp.zeros(l_scr.shape, F32)
    acc_scr[...] = jnp.zeros(acc_scr.shape, F32)

    def k_block(j):
        return k_ref[pl.ds(pl.multiple_of(j * tq, tq), tq), :]

    def vt_block(j):
        return vt_ref[:, pl.ds(pl.multiple_of(j * tq, tq), tq)]

    def produce_piece(dst, k, p, masked):
        s = lax.dot_general(k, qp[p], (((1,), (1,)), ((), ())), preferred_element_type=F32)
        if masked:
            kpos = lax.broadcasted_iota(I32, (tq, pw), 0)
            qpos = lax.broadcasted_iota(I32, (tq, pw), 1) + (p % ATTN_COL_SPLIT) * pw
            s = jnp.where(qpos >= kpos, s, NEG_INF)
        dst[p] = s

    def consume_piece(src, vt, p):
        s = src[p]
        m_prev = m_scr[p]
        m_new = jnp.maximum(m_prev, jnp.max(s, axis=0, keepdims=True))
        alpha = jnp.exp2(m_prev - m_new)
        e = jnp.exp2(s - m_new)
        l_scr[p] = alpha * l_scr[p] + jnp.sum(e, axis=0, keepdims=True)
        acc_scr[p] = alpha * acc_scr[p] + jnp.dot(vt, e.astype(BF16), preferred_element_type=F32)
        m_scr[p] = m_new

    def consume(src, kv_idx):
        vt = vt_block(kv_idx)
        for p in range(n_pieces):
            consume_piece(src, vt, p)

    def fused(src, kv_consume, dst, kv_produce):
        k = k_block(kv_produce)
        vt = vt_block(kv_consume)
        produce_piece(dst, k, 0, False)
        produce_piece(dst, k, 1, False)
        for p in range(n_pieces):
            consume_piece(src, vt, p)
            if p + 2 < n_pieces:
                produce_piece(dst, k, p + 2, False)

    k_diag = k_block(i)
    for p in range(n_pieces):
        produce_piece(s0_scr, k_diag, p, True)

    def body(u, carry):
        fused(s0_scr, jnp.where(u == 0, i, 2 * u - 1), s1_scr, 2 * u)
        fused(s1_scr, 2 * u, s0_scr, 2 * u + 1)
        return carry

    lax.fori_loop(0, i // 2, body, 0)
    odd = lax.rem(i, 2) == 1

    @pl.when(odd)
    def _():
        fused(s0_scr, jnp.where(i == 1, i, i - 2), s1_scr, i - 1)
        consume(s1_scr, i - 1)

    @pl.when(jnp.logical_not(odd))
    def _():
        consume(s0_scr, jnp.maximum(i - 1, 0))

    lam = lam_ref[0, 0]
    for g in range(GQA):
        for c in range(ATTN_COL_SPLIT):
            p1 = g * ATTN_COL_SPLIT + c
            p2 = (GQA + g) * ATTN_COL_SPLIT + c
            ot = acc_scr[p1] / l_scr[p1] - lam * (acc_scr[p2] / l_scr[p2])
            ot = ot * lax.rsqrt(jnp.mean(ot * ot, axis=0, keepdims=True) + EPS) * gsub_ref[...] * out_scale
            o_ref[c * pw:(c + 1) * pw, g * V_DIM:(g + 1) * V_DIM] = ot.T.astype(o_ref.dtype)


def _attn_prompt(lam, q, kb, vt, gsub_col, *, tq, out_scale):
    B, S, _ = q.shape
    n_pieces = 2 * GQA * ATTN_COL_SPLIT
    pw = tq // ATTN_COL_SPLIT
    return pl.pallas_call(
        functools.partial(_attn_prompt_kernel, tq=tq, out_scale=out_scale),
        grid=(B, N_KV_HEADS, S // tq),
        in_specs=[pl.BlockSpec(memory_space=pltpu.SMEM),
                  pl.BlockSpec((None, tq, GQA * LANES), lambda b, n, i: (b, i, n)),
                  pl.BlockSpec((None, S, LANES), lambda b, n, i: (b, 0, n)),
                  pl.BlockSpec((None, V_DIM, S), lambda b, n, i: (b, n, 0)),
                  pl.BlockSpec((V_DIM, 1), lambda b, n, i: (0, 0))],
        out_specs=pl.BlockSpec((None, tq, GQA * V_DIM), lambda b, n, i: (b, i, n)),
        out_shape=jax.ShapeDtypeStruct((B, S, N_HEADS * V_DIM), BF16),
        scratch_shapes=[pltpu.VMEM((n_pieces, 1, pw), F32), pltpu.VMEM((n_pieces, 1, pw), F32),
                        pltpu.VMEM((n_pieces, V_DIM, pw), F32),
                        pltpu.VMEM((n_pieces, tq, pw), F32), pltpu.VMEM((n_pieces, tq, pw), F32)],
        compiler_params=_params(("arbitrary", "arbitrary", "arbitrary")),
        name="attn_prompt",
    )(lam, q, kb, vt, gsub_col)


N_QROWS = 2 * N_HEADS
PAGE_ROWS = PAGE_SIZE * N_KV_HEADS
SAMPLE_LOOKAHEAD = 2


def _attn_sample_kernel(pt_ref, lam_ref, q_ref, kn_ref, vn_ref, gsub_ref, ck_ref, cv_ref, o_ref,
                        kbuf, vbuf, sems, m_scr, l_scr, acc_scr, *, pages, out_scale):
    n_slots = SAMPLE_LOOKAHEAD + 1
    b = pl.program_id(0)
    c = pl.program_id(1)
    n_c = pl.num_programs(1)
    step = b * n_c + c
    n_steps = pl.num_programs(0) * n_c
    slot = lax.rem(step, n_slots)

    def page_copies(s):
        bb = lax.div(s, n_c)
        cc = lax.rem(s, n_c)
        sl = lax.rem(s, n_slots)
        out = []
        for p in range(pages):
            page = pt_ref[bb, cc * pages + p]
            out.append(pltpu.make_async_copy(ck_ref.at[page], kbuf.at[sl, p], sems.at[sl]))
            out.append(pltpu.make_async_copy(cv_ref.at[page], vbuf.at[sl, p], sems.at[sl]))
        return out

    def start_all(copies):
        for n, cp in enumerate(copies):
            cp.start(priority=n % 2)

    for ahead in range(SAMPLE_LOOKAHEAD):
        @pl.when((step == 0) & (ahead < n_steps))
        def _():
            start_all(page_copies(jnp.int32(ahead)))

    @pl.when(step + SAMPLE_LOOKAHEAD < n_steps)
    def _():
        start_all(page_copies(step + SAMPLE_LOOKAHEAD))

    for cp in page_copies(step):
        cp.wait()

    @pl.when(c == 0)
    def _():
        m_scr[...] = jnp.full(m_scr.shape, NEG_INF, F32)
        l_scr[...] = jnp.zeros(l_scr.shape, F32)
        acc_scr[...] = jnp.zeros(acc_scr.shape, F32)

    q = q_ref[...]
    r = lax.broadcasted_iota(I32, (N_QROWS, PAGE_ROWS), 0)
    col = lax.broadcasted_iota(I32, (N_QROWS, PAGE_ROWS), 1)
    valid = (col & (N_KV_HEADS - 1)) == ((r & (N_HEADS - 1)) >> 1)
    k_pages = [kbuf[slot, p] for p in range(pages)]
    v_pages = [vbuf[slot, p] for p in range(pages)]
    scores = []
    for p in range(pages):
        s = lax.dot_general(q, k_pages[p].astype(BF16), (((1,), (1,)), ((), ())),
                            preferred_element_type=F32)
        scores.append(jnp.where(valid, s, NEG_INF))
    m_prev = m_scr[...]
    m_new = m_prev
    for s in scores:
        m_new = jnp.maximum(m_new, jnp.max(s, axis=-1, keepdims=True))
    alpha = jnp.exp2(m_prev - m_new)
    l_new = alpha * l_scr[...]
    acc = alpha * acc_scr[...]
    for p in range(pages):
        e = jnp.where(valid, jnp.exp2(scores[p] - m_new), 0.0)
        l_new = l_new + jnp.sum(e, axis=-1, keepdims=True)
        acc = acc + jnp.dot(e.astype(BF16), v_pages[p].astype(BF16), preferred_element_type=F32)
    m_scr[...] = m_new
    l_scr[...] = l_new
    acc_scr[...] = acc

    @pl.when(c == pl.num_programs(1) - 1)
    def _():
        s_self = jnp.sum(q.astype(F32) * kn_ref[...], axis=-1, keepdims=True)
        m_fin = jnp.maximum(m_new, s_self)
        a = jnp.exp2(m_new - m_fin)
        e_self = jnp.exp2(s_self - m_fin)
        o = (a * acc + e_self * vn_ref[...]) / (a * l_new + e_self)
        lam = lam_ref[0, 0]
        od = o[:N_HEADS] - lam * o[N_HEADS:]
        o_ref[...] = (_rmsnorm(od, gsub_ref[...]) * out_scale).astype(o_ref.dtype)


def _attn_sample(page_table, lam, q16, kn16, vn16, gsub, ck3, cv3, *, pages, out_scale):
    DB, n_pages = page_table.shape
    steps = n_pages // pages
    row3 = lambda n: pl.BlockSpec((None, n, LANES), lambda b, c, pt: (b, 0, 0))
    grid_spec = pltpu.PrefetchScalarGridSpec(
        num_scalar_prefetch=1,
        grid=(DB, steps),
        in_specs=[pl.BlockSpec(memory_space=pltpu.SMEM), row3(N_QROWS), row3(N_QROWS), row3(N_QROWS),
                  pl.BlockSpec((1, V_DIM), lambda b, c, pt: (0, 0)),
                  pl.BlockSpec(memory_space=pl.ANY), pl.BlockSpec(memory_space=pl.ANY)],
        out_specs=row3(N_HEADS),
        scratch_shapes=[pltpu.VMEM((SAMPLE_LOOKAHEAD + 1, pages, PAGE_ROWS, LANES), F32),
                        pltpu.VMEM((SAMPLE_LOOKAHEAD + 1, pages, PAGE_ROWS, LANES), F32),
                        pltpu.SemaphoreType.DMA((SAMPLE_LOOKAHEAD + 1,)),
                        pltpu.VMEM((N_QROWS, 1), F32), pltpu.VMEM((N_QROWS, 1), F32),
                        pltpu.VMEM((N_QROWS, V_DIM), F32)],
    )
    return pl.pallas_call(
        functools.partial(_attn_sample_kernel, pages=pages, out_scale=out_scale),
        grid_spec=grid_spec,
        out_shape=jax.ShapeDtypeStruct((DB, N_HEADS, V_DIM), BF16),
        compiler_params=_params(("arbitrary", "arbitrary")),
        name="attn_sample",
    )(page_table, lam, q16, kn16, vn16, gsub, ck3, cv3)


def _post_kernel(o_ref, sga_ref, gyb_ref, x_ref, gate_ref, shift_ref, scale_ref, g2_ref,
                 wba_ref, wout_ref, wr_ref, br_ref,
                 x1_ref, h2p_ref, route_ref, counts_ref, base_scr):
    tm = x_ref.shape[0]
    first = (pl.program_id(0) == 0) & (pl.program_id(1) == 0)

    @pl.when(first)
    def _():
        base_scr[...] = jnp.zeros(base_scr.shape, F32)

    ya = jnp.dot(o_ref[...], wba_ref[...], preferred_element_type=F32)
    merged = sga_ref[...].astype(F32) * ya + gyb_ref[...].astype(F32)
    upd = jnp.dot(merged.astype(BF16), wout_ref[...], preferred_element_type=F32)
    x1 = x_ref[...] + gate_ref[...] * upd
    x1_ref[...] = x1
    h2 = _rmsnorm(x1, g2_ref[...]) * (1.0 + scale_ref[...]) + shift_ref[...]
    half = D_MODEL // 2
    h2p_ref[...] = _pack_pair(h2[:, :half], h2[:, half:])

    h_hi = h2.astype(BF16)
    h_lo = (h2 - h_hi.astype(F32)).astype(BF16)
    logits = (jnp.dot(h_hi, wr_ref[0], preferred_element_type=F32)
              + (jnp.dot(h_hi, wr_ref[1], preferred_element_type=F32)
                 + jnp.dot(h_lo, wr_ref[0], preferred_element_type=F32))) + br_ref[...]
    lane = lax.broadcasted_iota(I32, (tm, LANES), 1)
    lanef = lane.astype(F32)
    big = float(LANES)
    is_g = lane < N_GROUPS
    lg = jnp.where(is_g, logits, -jnp.inf)
    gmax = jnp.max(lg, axis=-1, keepdims=True)
    g_idx = jnp.min(jnp.where(lg == gmax, lanef, big), axis=-1, keepdims=True)
    p_grp = 1.0 / jnp.sum(jnp.where(is_g, jnp.exp(logits - gmax), 0.0), axis=-1, keepdims=True)
    e_lo = N_GROUPS + EXPERTS_PER_GROUP * g_idx
    in_grp = (lanef >= e_lo) & (lanef < e_lo + EXPERTS_PER_GROUP)
    emax = jnp.max(jnp.where(in_grp, logits, -jnp.inf), axis=-1, keepdims=True)
    ee = jnp.where(in_grp, jnp.exp(logits - emax), 0.0)
    pe = ee / jnp.sum(ee, axis=-1, keepdims=True)
    pe_m = jnp.where(in_grp, pe, -1.0)
    v1 = jnp.max(pe_m, axis=-1, keepdims=True)
    i1 = jnp.min(jnp.where(pe_m == v1, lanef, big), axis=-1, keepdims=True)
    pe_m2 = jnp.where(lanef == i1, -1.0, pe_m)
    v2 = jnp.max(pe_m2, axis=-1, keepdims=True)
    i2 = jnp.min(jnp.where(pe_m2 == v2, lanef, big), axis=-1, keepdims=True)
    tsum = v1 + v2
    w1 = p_grp * (v1 / tsum)
    w2 = p_grp * (v2 / tsum)
    e1 = i1 - N_GROUPS
    e2 = i2 - N_GROUPS

    oh1 = (lanef == e1).astype(F32)
    oh2 = (lanef == e2).astype(F32)
    ohs = oh1 + oh2
    rr = lax.broadcasted_iota(I32, (tm, tm), 0)
    cc = lax.broadcasted_iota(I32, (tm, tm), 1)
    tri = jnp.where(rr > cc, 1.0, 0.0).astype(BF16)
    before = jnp.dot(tri, ohs.astype(BF16), preferred_element_type=F32) + base_scr[...]
    r1 = jnp.sum(before * oh1, axis=-1, keepdims=True)
    r2 = jnp.sum(before * oh2, axis=-1, keepdims=True)
    new_base = base_scr[...] + jnp.sum(ohs, axis=0, keepdims=True)
    base_scr[...] = new_base
    counts_ref[...] = jnp.broadcast_to(new_base, counts_ref.shape)

    route = jnp.zeros((tm, LANES), F32)
    for idx, val in enumerate((e1, e2, r1, r2, w1, w2)):
        route = jnp.where(lane == idx, val, route)
    route_ref[...] = route


def _post(o, sga, gyb, x3, gate, shift, scale, g2, wba, wout, wr, br, *, tm):
    B, S, D = x3.shape
    per_token = gate.shape[1] != 1
    mod_spec = (pl.BlockSpec((None, tm, D), lambda b, i: (b, i, 0)) if per_token
                else pl.BlockSpec((None, 1, D), lambda b, i: (b, 0, 0)))
    const2 = lambda a: pl.BlockSpec(a.shape, lambda b, i: (0,) * a.ndim, pipeline_mode=pl.Buffered(1))
    tok = lambda n: pl.BlockSpec((None, tm, n), lambda b, i: (b, i, 0))
    return pl.pallas_call(
        _post_kernel,
        grid=(B, S // tm),
        in_specs=[tok(D), tok(D), tok(D), tok(D), mod_spec, mod_spec, mod_spec, const2(g2),
                  const2(wba), const2(wout), const2(wr), const2(br)],
        out_specs=[tok(D), tok(D // 2), tok(LANES), pl.BlockSpec((8, LANES), lambda b, i: (0, 0))],
        out_shape=[jax.ShapeDtypeStruct((B, S, D), F32),
                   jax.ShapeDtypeStruct((B, S, D // 2), I32),
                   jax.ShapeDtypeStruct((B, S, LANES), F32),
                   jax.ShapeDtypeStruct((8, LANES), F32)],
        scratch_shapes=[pltpu.VMEM((1, LANES), F32)],
        compiler_params=_params(("arbitrary", "arbitrary")),
        name="post",
    )(o, sga, gyb, x3, gate, shift, scale, g2, wba, wout, wr, br)


ROW_UNROLL = 8


def _row_copy(src, dst, src_row, dst_row, sem):
    return pltpu.make_async_copy(src.at[pl.ds(src_row, 1)], dst.at[pl.ds(dst_row, 1)], sem)


def _dispatch_kernel(dest_ref, h_ref, xs_in_ref, xs_ref, dest_smem, sem, dsem):
    del xs_in_ref
    tm = h_ref.shape[0]
    cp = pltpu.make_async_copy(dest_ref.at[0], dest_smem, dsem)
    cp.start()
    cp.wait()

    def issue(blk, carry):
        for u in range(ROW_UNROLL):
            r = blk * ROW_UNROLL + u
            _row_copy(h_ref, xs_ref, r, dest_smem[0, 2 * r], sem).start(priority=0)
            _row_copy(h_ref, xs_ref, r, dest_smem[0, 2 * r + 1], sem).start(priority=1)
        return carry

    lax.fori_loop(0, tm // ROW_UNROLL, issue, 0)
    for _ in range(2):
        pltpu.make_async_copy(h_ref, xs_ref.at[pl.ds(0, tm)], sem).wait()


def _dispatch(dest, h2p, slots, *, tm):
    T, W = h2p.shape
    nt = T // tm
    xs0 = jnp.zeros((slots, W), I32)
    return pl.pallas_call(
        _dispatch_kernel,
        grid=(nt,),
        in_specs=[pl.BlockSpec((1, 1, 2 * tm), lambda i: (i, 0, 0)),
                  pl.BlockSpec((tm, W), lambda i: (i, 0)),
                  pl.BlockSpec(memory_space=pl.ANY)],
        out_specs=pl.BlockSpec(memory_space=pl.ANY),
        out_shape=jax.ShapeDtypeStruct((slots, W), I32),
        scratch_shapes=[pltpu.SMEM((1, 2 * tm), I32), pltpu.SemaphoreType.DMA(()), pltpu.SemaphoreType.DMA(())],
        input_output_aliases={2: 0},
        compiler_params=_params(("arbitrary",)),
        name="dispatch",
    )(dest.reshape(nt, 1, 2 * tm), h2p, xs0)


SC_GATHER_ROWS = 64


def _sc_gather(table, idx):
    info = plsc.get_sparse_core_info()
    n_workers = info.num_cores * info.num_subcores
    B = idx.shape[0]
    W = table.shape[1]
    R = SC_GATHER_ROWS
    per_worker = B // n_workers
    n_chunks = per_worker // R
    assert per_worker * n_workers == B and n_chunks * R == per_worker and n_chunks % 2 == 0
    mesh = plsc.VectorSubcoreMesh(core_axis_name="c", subcore_axis_name="s")

    def body(table_hbm, idx_hbm, out_hbm, idx_v, rows_v, sems):
        wid = lax.axis_index("s") * info.num_cores + lax.axis_index("c")
        base = wid * per_worker

        def gather(chunk, slot):
            off = pl.multiple_of(base + chunk * R, 8)
            pltpu.sync_copy(idx_hbm.at[pl.ds(off, R)], idx_v.at[slot])
            return pltpu.make_async_copy(table_hbm.at[idx_v.at[slot]], rows_v.at[slot], sems.at[slot])

        def flush(chunk, slot):
            off = pl.multiple_of(base + chunk * R, 8)
            pltpu.sync_copy(rows_v.at[slot], out_hbm.at[pl.ds(off, R)])

        gather(0, 0).start()

        @pl.loop(0, n_chunks, step=2)
        def _(g):
            gather(g + 1, 1).start()
            pltpu.make_async_copy(table_hbm.at[idx_v.at[0]], rows_v.at[0], sems.at[0]).wait()
            flush(g, 0)

            @pl.when(g + 2 < n_chunks)
            def _():
                gather(g + 2, 0).start()

            pltpu.make_async_copy(table_hbm.at[idx_v.at[1]], rows_v.at[1], sems.at[1]).wait()
            flush(g + 1, 1)

    return pl.kernel(
        body,
        out_type=jax.ShapeDtypeStruct((B, W), I32),
        mesh=mesh,
        scratch_types=[pltpu.VMEM((2, R), I32), pltpu.VMEM((2, R, W), I32), pltpu.SemaphoreType.DMA((2,))],
        name="sc_gather",
    )(table, idx)


def _ffn_kernel(be_ref, nu_ref, x_ref, wg_ref, wu_ref, wd_ref, y_ref):
    del be_ref
    i = pl.program_id(0)

    @pl.when(i < nu_ref[0])
    def _():
        a, b = _unpack_pair(x_ref[...])
        x = jnp.concatenate([a.astype(BF16), b.astype(BF16)], axis=1)
        hg = jnp.dot(x, wg_ref[...].astype(BF16), preferred_element_type=F32)
        hu = jnp.dot(x, wu_ref[...].astype(BF16), preferred_element_type=F32)
        hid = (hg * jax.nn.sigmoid(hg)) * hu
        y = jnp.dot(hid.astype(BF16), wd_ref[...].astype(BF16), preferred_element_type=F32)
        half = D_MODEL // 2
        y_ref[...] = _pack_pair(y[:, :half], y[:, half:])

    @pl.when(i >= nu_ref[0])
    def _():
        y_ref[...] = jnp.zeros(y_ref.shape, I32)


def _ffn(block_expert, n_used, xs, weg, weu, wed, *, bm):
    slots, W = xs.shape
    nb = slots // bm
    grid_spec = pltpu.PrefetchScalarGridSpec(
        num_scalar_prefetch=2,
        grid=(nb,),
        in_specs=[pl.BlockSpec((bm, W), lambda i, be, nu: (i, 0)),
                  pl.BlockSpec((None, D_MODEL, D_EXPERT), lambda i, be, nu: (be[i], 0, 0)),
                  pl.BlockSpec((None, D_MODEL, D_EXPERT), lambda i, be, nu: (be[i], 0, 0)),
                  pl.BlockSpec((None, D_EXPERT, D_MODEL), lambda i, be, nu: (be[i], 0, 0))],
        out_specs=pl.BlockSpec((bm, W), lambda i, be, nu: (i, 0)),
    )
    return pl.pallas_call(
        _ffn_kernel,
        grid_spec=grid_spec,
        out_shape=jax.ShapeDtypeStruct((slots, W), I32),
        compiler_params=_params(("arbitrary",)),
        name="ffn",
    )(block_expert, n_used, xs, weg, weu, wed)


def _combine_kernel(dest_ref, dest_next_ref, ys_ref, x1_ref, route_ref, gate_ref, fshift_ref, fscale_ref,
                    gf_ref, o_ref, dest_smem, rows_scr, sems, dsem):
    tm = x1_ref.shape[0]
    step = pl.program_id(0) * pl.num_programs(1) + pl.program_id(1)
    n_steps = pl.num_programs(0) * pl.num_programs(1)
    slot = lax.rem(step, 2)

    def gather(dref, sl):
        cp = pltpu.make_async_copy(dref.at[0], dest_smem, dsem)
        cp.start()
        cp.wait()

        def issue(blk, carry):
            for u in range(ROW_UNROLL):
                r = blk * ROW_UNROLL + u
                _row_copy(ys_ref, rows_scr.at[sl, 0], dest_smem[0, 2 * r], r, sems.at[sl]).start(priority=0)
                _row_copy(ys_ref, rows_scr.at[sl, 1], dest_smem[0, 2 * r + 1], r, sems.at[sl]).start(priority=1)
            return carry

        lax.fori_loop(0, tm // ROW_UNROLL, issue, 0)

    @pl.when(step == 0)
    def _():
        gather(dest_ref, 0)

    @pl.when(step + 1 < n_steps)
    def _():
        gather(dest_next_ref, 1 - slot)

    for half in range(2):
        pltpu.make_async_copy(ys_ref.at[pl.ds(0, tm)], rows_scr.at[slot, half], sems.at[slot]).wait()

    o_ref[...] = _combine_math(rows_scr[slot, 0], rows_scr[slot, 1], x1_ref[...], route_ref[...],
                               gate_ref[...], fshift_ref[...], fscale_ref[...], gf_ref[...])


def _combine_math(rows1, rows2, x1, route, gate, fshift, fscale, gf):
    w1 = route[:, 4:5]
    w2 = route[:, 5:6]
    a1, b1 = _unpack_pair(rows1)
    a2, b2 = _unpack_pair(rows2)
    y = jnp.concatenate([a1 * w1 + a2 * w2, b1 * w1 + b2 * w2], axis=1)
    x2 = x1 + gate * y
    return _rmsnorm(x2, gf) * (1.0 + fscale) + fshift


def _combine_dense_kernel(rows_ref, x1_ref, route_ref, gate_ref, fshift_ref, fscale_ref, gf_ref, o_ref):
    o_ref[...] = _combine_math(rows_ref[0], rows_ref[1], x1_ref[...], route_ref[...],
                               gate_ref[...], fshift_ref[...], fscale_ref[...], gf_ref[...])


def _combine_dense(rows, x1, route, gate, fshift, fscale, gf, *, tm):
    B, S, D = x1.shape
    W = rows.shape[-1]
    per_token = gate.shape[1] != 1
    mod_spec = (pl.BlockSpec((None, tm, D), lambda b, i: (b, i, 0)) if per_token
                else pl.BlockSpec((None, 1, D), lambda b, i: (b, 0, 0)))
    tok = lambda n: pl.BlockSpec((None, tm, n), lambda b, i: (b, i, 0))
    return pl.pallas_call(
        _combine_dense_kernel,
        grid=(B, S // tm),
        in_specs=[pl.BlockSpec((2, None, tm, W), lambda b, i: (0, b, i, 0)),
                  tok(D), tok(LANES), mod_spec, mod_spec, mod_spec,
                  pl.BlockSpec(gf.shape, lambda b, i: (0, 0))],
        out_specs=tok(D),
        out_shape=jax.ShapeDtypeStruct((B, S, D), F32),
        compiler_params=_params(("arbitrary", "arbitrary")),
        name="combine_dense",
    )(rows, x1, route, gate, fshift, fscale, gf)


def _combine(dest, ys, x1, route, gate, fshift, fscale, gf, *, tm):
    B, S, D = x1.shape
    nt = S // tm
    W = ys.shape[1]
    per_token = gate.shape[1] != 1
    mod_spec = (pl.BlockSpec((None, tm, D), lambda b, i: (b, i, 0)) if per_token
                else pl.BlockSpec((None, 1, D), lambda b, i: (b, 0, 0)))
    tok = lambda n: pl.BlockSpec((None, tm, n), lambda b, i: (b, i, 0))
    dest3 = dest.reshape(B * nt, 1, 2 * tm)
    return pl.pallas_call(
        _combine_kernel,
        grid=(B, nt),
        in_specs=[pl.BlockSpec((1, 1, 2 * tm), lambda b, i: (b * nt + i, 0, 0)),
                  pl.BlockSpec((1, 1, 2 * tm), lambda b, i: (jnp.minimum(b * nt + i + 1, B * nt - 1), 0, 0)),
                  pl.BlockSpec(memory_space=pl.ANY),
                  tok(D), tok(LANES), mod_spec, mod_spec, mod_spec,
                  pl.BlockSpec(gf.shape, lambda b, i: (0, 0))],
        out_specs=tok(D),
        out_shape=jax.ShapeDtypeStruct((B, S, D), F32),
        scratch_shapes=[pltpu.SMEM((1, 2 * tm), I32), pltpu.VMEM((2, 2, tm, W), I32),
                        pltpu.SemaphoreType.DMA((2,)), pltpu.SemaphoreType.DMA(())],
        compiler_params=_params(("arbitrary", "arbitrary")),
        name="combine",
    )(dest3, dest3, ys, x1, route, gate, fshift, fscale, gf)


def _rope_tables(pos):
    half = ROPE_DIM // 2
    inv_freq = ROPE_THETA ** (-jnp.arange(0, ROPE_DIM, 2, dtype=F32) / ROPE_DIM)
    ang = pos.astype(F32)[:, None] * inv_freq[None, :]
    cos, sin = jnp.cos(ang), jnp.sin(ang)
    n = pos.shape[0]
    rest = HEAD_DIM - ROPE_DIM
    c64 = jnp.concatenate([cos, cos, jnp.ones((n, rest), F32)], axis=1)
    a64 = jnp.concatenate([-sin, jnp.zeros((n, HEAD_DIM - half), F32)], axis=1)
    b64 = jnp.concatenate([jnp.zeros((n, half), F32), sin, jnp.zeros((n, rest), F32)], axis=1)
    return tuple(jnp.tile(t, (1, LANES // HEAD_DIM)) for t in (c64, a64, b64))


def _moe_plan(route, counts, bm):
    T = route.shape[0]
    e = route[:, 0:2].astype(I32)
    rank = route[:, 2:4].astype(I32)
    cnt = counts[0, :N_EXPERTS].astype(I32)
    padded = (cnt + bm - 1) // bm * bm
    pad_end = jnp.cumsum(padded)
    pad_start = pad_end - padded
    ids = jnp.arange(N_EXPERTS, dtype=I32)
    dest = rank + jnp.sum(jnp.where(e[..., None] == ids, pad_start, 0), axis=-1)
    n_blocks = -(-(2 * T) // bm) + N_EXPERTS
    block_lo = jnp.arange(n_blocks, dtype=I32) * bm
    block_expert = jnp.minimum(jnp.sum((pad_end[None, :] <= block_lo[:, None]).astype(I32), axis=1),
                               N_EXPERTS - 1)
    n_used = (pad_end[-1:] // bm).astype(I32)
    return dest.reshape(-1), block_expert, n_used, n_blocks * bm


def _slot_tokens(route, counts, bm, slots):
    T = route.shape[0]
    e_flat = route[:, 0:2].astype(I32).reshape(-1)
    cnt = counts[0, :N_EXPERTS].astype(I32)
    padded = (cnt + bm - 1) // bm * bm
    pad_end = jnp.cumsum(padded)
    pad_start = pad_end - padded
    start = jnp.cumsum(cnt) - cnt
    order = jnp.argsort(e_flat, stable=True).astype(I32)
    s = jnp.arange(slots, dtype=I32)
    eb = jnp.minimum(jnp.sum((pad_end[None, :] <= s[:, None]).astype(I32), axis=1), N_EXPERTS - 1)
    hot = eb[:, None] == jnp.arange(N_EXPERTS, dtype=I32)
    pick = lambda tab: jnp.sum(jnp.where(hot, tab, 0), axis=1)
    r = s - pick(pad_start)
    valid = r < pick(cnt)
    src = jnp.clip(pick(start) + r, 0, 2 * T - 1)
    return jnp.where(valid, order[src] // 2, s % T)


def _layer(x3, mod, modf, pos_tabs, attend, lam_init, wts, *, tm, tmr, bm, chunked, emit_gv, sc_rows):
    B, S, D = x3.shape
    m = lambda k: mod[:, :, k, :]
    outs = _inproj(x3, m(0), m(1), wts['g1'], wts['win'], pos_tabs, wts['wsp'], wts['bsp'],
                   wts['gln'], wts['bln'], wts['wbb'], tm=tm, chunked=chunked, emit_gv=emit_gv)
    q, kf, vf, kb, vt, sga, gyb = outs[:7]
    gv = outs[7] if emit_gv else None
    o = attend(q, kf, vf, kb, vt)
    x1, h2p, route, counts = _post(o, sga, gyb, x3, m(2), m(3), m(4), wts['g2'], wts['wba'],
                                   wts['wout'], wts['wr'], wts['br'], tm=tm)
    T = B * S
    W = D // 2
    dest, block_expert, n_used, slots = _moe_plan(route.reshape(T, LANES), counts, bm)
    fin = (m(5), modf[:, :, 0, :], modf[:, :, 1, :], wts['gf'])
    if sc_rows:
        xs = _sc_gather(h2p.reshape(T, W), _slot_tokens(route.reshape(T, LANES), counts, bm, slots))
        ys = _ffn(block_expert, n_used, xs, wts['weg'], wts['weu'], wts['wed'], bm=bm)
        rows = _sc_gather(ys, dest.reshape(T, 2).T.reshape(-1)).reshape(2, B, S, W)
        y = _combine_dense(rows, x1, route, *fin, tm=tm)
    else:
        xs = _dispatch(dest, h2p.reshape(T, W), slots, tm=tmr)
        ys = _ffn(block_expert, n_used, xs, wts['weg'], wts['weu'], wts['wed'], bm=bm)
        y = _combine(dest, ys, x1, route, *fin, tm=tmr)
    return y, kf, vf, gv


def kernel(x_prompt, x_sample, cache_k, cache_v, page_table, c_prompt, c_sample, w_ada, b_ada, w_ada_final, b_ada_final, g_norm1, g_norm2, g_final, w_in, lambda_q1, lambda_k1, lambda_q2, lambda_k2, g_subln, g_gmlp_ln, b_gmlp_ln, w_spatial, b_spatial, w_branch_a, w_branch_b, w_out, w_router_group, b_router_group, w_router_expert, b_router_expert, w_exp_gate, w_exp_up, w_exp_down):
    depth = w_ada.shape[0]
    assert depth == 1
    B, S, D = x_prompt.shape
    DB, DS, _ = x_sample.shape
    assert DS == 1 and D == D_MODEL
    n_pages = page_table.shape[1]
    n_past = n_pages * PAGE_SIZE
    l = 0
    lam_init = 0.8 - 0.6 * math.exp(-0.3 * l)
    out_scale = 1.0 - lam_init

    c_all = jnp.concatenate([c_prompt, c_sample], axis=0)
    mod_all = _adaln(c_all, w_ada[l], b_ada[l])
    modf_all = _adaln(c_all, w_ada_final, b_ada_final)
    mod_p = mod_all[:B].reshape(B, 1, 6, D)
    mod_s = mod_all[B:].reshape(1, DB, 6, D)
    modf_p = modf_all[:B].reshape(B, 1, 2, D)
    modf_s = modf_all[B:].reshape(1, DB, 2, D)
    lam = _lam(lambda_q1[l], lambda_k1[l], lambda_q2[l], lambda_k2[l], lam_init)[0:1, 0:1]

    row = lambda a: a.reshape(1, -1)
    wr = jnp.zeros((D, LANES), F32)
    wr = wr.at[:, :N_GROUPS].set(w_router_group[l]).at[:, N_GROUPS:N_GROUPS + N_EXPERTS].set(w_router_expert[l])
    wr_hi = wr.astype(BF16)
    wr_lo = (wr - wr_hi.astype(F32)).astype(BF16)
    br = jnp.zeros((1, LANES), F32)
    br = br.at[0, :N_GROUPS].set(b_router_group[l]).at[0, N_GROUPS:N_GROUPS + N_EXPERTS].set(b_router_expert[l])
    wts = dict(
        g1=row(g_norm1[l]), g2=row(g_norm2[l]), gf=row(g_final), win=w_in[l].astype(BF16),
        gln=row(g_gmlp_ln[l]), bln=row(b_gmlp_ln[l]), wbb=w_branch_b[l].astype(BF16),
        wba=w_branch_a[l].astype(BF16), wout=w_out[l].astype(BF16), wr=jnp.stack([wr_hi, wr_lo]), br=br,
        weg=w_exp_gate[l], weu=w_exp_up[l], wed=w_exp_down[l])
    gsub = row(g_subln[l])

    wts_s = dict(wts, wsp=row(jnp.repeat(w_spatial[l][:, 0, 0], GMLP_GROUP_DIM)),
                 bsp=row(jnp.repeat(b_spatial[l][:, 0], GMLP_GROUP_DIM)))
    tabs_s = _rope_tables(jnp.full((1,), n_past, dtype=I32))
    ck3 = cache_k[l].reshape(-1, PAGE_ROWS, LANES)
    cv3 = cache_v[l].reshape(-1, PAGE_ROWS, LANES)
    half_mask = (jnp.arange(LANES)[None, :] // HEAD_DIM == jnp.arange(2)[:, None])

    def attend_s(q, kf, vf, kb, vb):
        del kb, vb
        qh = q.reshape(DB, 1, N_HEADS, LANES)
        q16 = jnp.where(half_mask[None, :, None, :], qh, jnp.zeros_like(qh)).reshape(DB, N_QROWS, LANES)
        kv_of_row = (jnp.arange(N_QROWS) % N_HEADS) // GQA
        kn16 = kf.reshape(DB, N_KV_HEADS, LANES)[:, kv_of_row]
        vn16 = vf.reshape(DB, N_KV_HEADS, V_DIM)[:, kv_of_row]
        pages = 8 if n_pages % 8 == 0 else (4 if n_pages % 4 == 0 else 1)
        o = _attn_sample(page_table, lam, q16, kn16, vn16, gsub, ck3, cv3, pages=pages, out_scale=out_scale)
        return o.reshape(1, DB, N_HEADS * V_DIM)

    y_s, k_s, v_s, gv_s = _layer(x_sample.reshape(1, DB, D), mod_s, modf_s, tabs_s, attend_s, lam_init,
                                 wts_s, tm=DB, tmr=DB, bm=128, chunked=False, emit_gv=True, sc_rows=False)

    wts_p = dict(wts, wsp=w_spatial[l],
                 bsp=jnp.broadcast_to(b_spatial[l][:, :, None], (GMLP_GROUPS, CHUNK, GMLP_GROUP_DIM)))
    tabs_p = _rope_tables(jnp.arange(S, dtype=I32))
    tq = min(512, S)
    gsub_col = g_subln[l].reshape(V_DIM, 1)
    attend_p = lambda q, kf, vf, kb, vt: _attn_prompt(lam, q, kb, vt, gsub_col, tq=tq, out_scale=out_scale)
    y_p, k_p, v_p, _ = _layer(x_prompt, mod_p, modf_p, tabs_p, attend_p, lam_init, wts_p,
                              tm=min(512, S), tmr=min(256, S), bm=512, chunked=True, emit_gv=False,
                              sc_rows=True)

    return (y_p,
            y_s.reshape(DB, 1, D),
            k_p.reshape(1, B, S, N_KV_HEADS, 2 * HEAD_DIM),
            v_p.reshape(1, B, S, N_KV_HEADS, V_DIM),
            k_s.reshape(1, DB, 1, N_KV_HEADS, 2 * HEAD_DIM),
            v_s.reshape(1, DB, 1, N_KV_HEADS, V_DIM),
            gv_s.reshape(1, DB, 1, D))
```

```python
import functools
import math

import jax
import jax.numpy as jnp
from jax import lax
from jax.experimental import pallas as pl
from jax.experimental.pallas import tpu as pltpu
from jax.experimental.pallas import tpu_sc as plsc

F32 = jnp.float32
BF16 = jnp.bfloat16
U32 = jnp.uint32
I32 = jnp.int32

D_MODEL = 1024
N_HEADS = 8
N_KV_HEADS = 4
GQA = N_HEADS // N_KV_HEADS
HEAD_DIM = 64
V_DIM = 128
ROPE_DIM = 16
ROPE_THETA = 500000.0
PAGE_SIZE = 128
GMLP_GROUPS = 8
GMLP_GROUP_DIM = 128
CHUNK = 128
N_GROUPS = 4
EXPERTS_PER_GROUP = 8
N_EXPERTS = 32
D_EXPERT = 512
Q_COLS = 1024
K_COLS = 512
V_COLS = 512
COL_Q, COL_K, COL_V, COL_GU, COL_GV, COL_GA, COL_GB = 0, 1024, 1536, 2048, 3072, 4096, 5120
EPS = 1e-6
NEG_INF = -1e30
LANES = 128
VMEM_LIMIT = 56 * 1024 * 1024

HIGHEST = lax.Precision.HIGHEST
Q_SCALE = HEAD_DIM ** -0.5 * math.log2(math.e)


def _params(sem, flags=None):
    return pltpu.CompilerParams(dimension_semantics=sem, vmem_limit_bytes=VMEM_LIMIT, flags=flags)


def _rmsnorm(x, g):
    return x * lax.rsqrt(jnp.mean(x * x, axis=-1, keepdims=True) + EPS) * g


def _pack_pair(a, b):
    ua = lax.bitcast_convert_type(a.astype(BF16).astype(F32), U32)
    ub = lax.bitcast_convert_type(b.astype(BF16).astype(F32), U32)
    return lax.bitcast_convert_type((ua >> 16) | (ub & jnp.uint32(0xFFFF0000)), I32)


def _unpack_pair(w):
    w = lax.bitcast_convert_type(w, U32)
    a = lax.bitcast_convert_type(w << 16, F32)
    b = lax.bitcast_convert_type(w & jnp.uint32(0xFFFF0000), F32)
    return a, b


def _adaln_kernel(c_ref, w_ref, b_ref, o_ref):
    c = c_ref[...]
    a = c * jax.nn.sigmoid(c)
    o_ref[...] = jnp.dot(a, w_ref[...], preferred_element_type=F32, precision=HIGHEST) + b_ref[...]


def _adaln(c, w, b, tn=512):
    rows, d = c.shape
    n = w.shape[1]
    return pl.pallas_call(
        _adaln_kernel,
        grid=(n // tn,),
        in_specs=[pl.BlockSpec((rows, d), lambda j: (0, 0)),
                  pl.BlockSpec((d, tn), lambda j: (0, j)),
                  pl.BlockSpec((1, tn), lambda j: (0, j))],
        out_specs=pl.BlockSpec((rows, tn), lambda j: (0, j)),
        out_shape=jax.ShapeDtypeStruct((rows, n), F32),
        compiler_params=_params(("arbitrary",)),
        name="adaln",
    )(c, w, b.reshape(1, n))


def _lam_kernel(q1_ref, k1_ref, q2_ref, k2_ref, o_ref, *, lam_init):
    a = jnp.sum(q1_ref[...] * k1_ref[...], axis=-1, keepdims=True)
    b = jnp.sum(q2_ref[...] * k2_ref[...], axis=-1, keepdims=True)
    lam = jnp.exp(a) - jnp.exp(b) + lam_init
    o_ref[...] = jnp.broadcast_to(lam, o_ref.shape)


def _lam(q1, k1, q2, k2, lam_init):
    args = [a.reshape(1, HEAD_DIM) for a in (q1, k1, q2, k2)]
    return pl.pallas_call(
        functools.partial(_lam_kernel, lam_init=lam_init),
        out_shape=jax.ShapeDtypeStruct((8, LANES), F32),
        name="lam",
    )(*args)


def _inproj_kernel(x_ref, shift_ref, scale_ref, g1_ref, win_ref, rc_ref, ra_ref, rb_ref,
                   wsp_ref, bsp_ref, gln_ref, bln_ref, wbb_ref,
                   q_ref, kf_ref, vf_ref, kb_ref, vt_ref, sga_ref, gyb_ref, *rest,
                   chunked, emit_gv):
    if emit_gv:
        gv_ref, sg_scr = rest
    else:
        (sg_scr,) = rest
    tm = x_ref.shape[0]
    x = x_ref[...]
    h = _rmsnorm(x, g1_ref[...]) * (1.0 + scale_ref[...]) + shift_ref[...]
    hb = h.astype(BF16)

    def proj(c0, n):
        return jnp.dot(hb, win_ref[:, c0:c0 + n], preferred_element_type=F32)

    rc, ra, rb = rc_ref[...], ra_ref[...], rb_ref[...]

    def rope(z):
        return z * rc + pltpu.roll(z, LANES - ROPE_DIM // 2, 1) * ra + pltpu.roll(z, ROPE_DIM // 2, 1) * rb

    zq = proj(COL_Q, Q_COLS)
    for j in range(Q_COLS // LANES):
        z = rope(zq[:, j * LANES:(j + 1) * LANES])
        q_ref[:, j * LANES:(j + 1) * LANES] = (z * Q_SCALE).astype(BF16)
    zk = proj(COL_K, K_COLS)
    for j in range(N_KV_HEADS):
        z = rope(zk[:, j * LANES:(j + 1) * LANES])
        kf_ref[pl.ds(j, tm, stride=N_KV_HEADS), :] = z
        kb_ref[:, j * LANES:(j + 1) * LANES] = z.astype(BF16)
    zv = proj(COL_V, V_COLS)
    for j in range(N_KV_HEADS):
        vf_ref[pl.ds(j, tm, stride=N_KV_HEADS), :] = zv[:, j * V_DIM:(j + 1) * V_DIM]
    vt_ref[...] = zv.T.astype(BF16)

    u = jax.nn.gelu(proj(COL_GU, D_MODEL))
    gv = jax.nn.gelu(proj(COL_GV, D_MODEL))
    xc = gv - jnp.mean(gv, axis=-1, keepdims=True)
    vr = xc * lax.rsqrt(jnp.mean(xc * xc, axis=-1, keepdims=True) + EPS) * gln_ref[...] + bln_ref[...]
    if emit_gv:
        gv_ref[...] = vr
    if chunked:
        vr16 = vr.astype(BF16)
        row = lax.broadcasted_iota(I32, (CHUNK, CHUNK), 0)
        col = lax.broadcasted_iota(I32, (CHUNK, CHUNK), 1)
        for g in range(GMLP_GROUPS):
            w = jnp.where(row >= col, wsp_ref[g], 0.0).astype(BF16)
            gs = slice(g * GMLP_GROUP_DIM, (g + 1) * GMLP_GROUP_DIM)
            for c in range(tm // CHUNK):
                rs = slice(c * CHUNK, (c + 1) * CHUNK)
                sp = jnp.dot(w, vr16[rs, gs], preferred_element_type=F32) + bsp_ref[g]
                sg_scr[rs, gs] = (u[rs, gs] * sp).astype(BF16)
    else:
        sg_scr[...] = (u * (vr * wsp_ref[...] + bsp_ref[...])).astype(BF16)
    yb = jnp.dot(sg_scr[...], wbb_ref[...], preferred_element_type=F32)
    gyb_ref[...] = (jax.nn.sigmoid(proj(COL_GB, D_MODEL)) * yb).astype(BF16)
    sga_ref[...] = jax.nn.sigmoid(proj(COL_GA, D_MODEL)).astype(BF16)


def _inproj(x3, shift, scale, g1, win, rope_tabs, wsp, bsp, gln, bln, wbb, *, tm, chunked, emit_gv):
    B, S, D = x3.shape
    per_token = shift.shape[1] != 1
    nt = S // tm
    mod_spec = (pl.BlockSpec((None, tm, D), lambda b, i: (b, i, 0)) if per_token
                else pl.BlockSpec((None, 1, D), lambda b, i: (b, 0, 0)))
    rope_rows = rope_tabs[0].shape[0]
    rope_spec = (pl.BlockSpec((tm, LANES), lambda b, i: (i, 0)) if rope_rows != 1
                 else pl.BlockSpec((1, LANES), lambda b, i: (0, 0)))
    const2 = lambda a: pl.BlockSpec(a.shape, lambda b, i: (0,) * a.ndim, pipeline_mode=pl.Buffered(1))
    tok = lambda n: pl.BlockSpec((None, tm, n), lambda b, i: (b, i, 0))
    head_rows = pl.BlockSpec((None, tm * N_KV_HEADS, LANES), lambda b, i: (b, i, 0))
    out_shapes = [jax.ShapeDtypeStruct((B, S, Q_COLS), BF16),
                  jax.ShapeDtypeStruct((B, S * N_KV_HEADS, LANES), F32),
                  jax.ShapeDtypeStruct((B, S * N_KV_HEADS, V_DIM), F32),
                  jax.ShapeDtypeStruct((B, S, K_COLS), BF16),
                  jax.ShapeDtypeStruct((B, V_COLS, S), BF16),
                  jax.ShapeDtypeStruct((B, S, D), BF16),
                  jax.ShapeDtypeStruct((B, S, D), BF16)]
    out_specs = [tok(Q_COLS), head_rows, head_rows, tok(K_COLS),
                 pl.BlockSpec((None, V_COLS, tm), lambda b, i: (b, 0, i)), tok(D), tok(D)]
    if emit_gv:
        out_shapes.append(jax.ShapeDtypeStruct((B, S, D), F32))
        out_specs.append(tok(D))
    return pl.pallas_call(
        functools.partial(_inproj_kernel, chunked=chunked, emit_gv=emit_gv),
        grid=(B, nt),
        in_specs=[tok(D), mod_spec, mod_spec, const2(g1), const2(win),
                  rope_spec, rope_spec, rope_spec, const2(wsp), const2(bsp),
                  const2(gln), const2(bln), const2(wbb)],
        out_specs=out_specs,
        out_shape=out_shapes,
        scratch_shapes=[pltpu.VMEM((tm, D), BF16)],
        compiler_params=_params(("arbitrary", "arbitrary")),
        name="inproj",
    )(x3, shift, scale, g1, win, *rope_tabs, wsp, bsp, gln, bln, wbb)


ATTN_COL_SPLIT = 2


def _attn_prompt_kernel(lam_ref, q_ref, k_ref, vt_ref, gsub_ref, o_ref, m_scr, l_scr, acc_scr,
                        s0_scr, s1_scr, *, tq, out_scale):
    i = pl.program_id(2)
    q = q_ref[...]
    lane = lax.broadcasted_iota(I32, (tq, LANES), 1)
    lo = lane < HEAD_DIM
    zero = jnp.zeros((tq, LANES), BF16)
    heads = [q[:, g * LANES:(g + 1) * LANES] for g in range(GQA)]
    pw = tq // ATTN_COL_SPLIT
    masked_heads = [jnp.where(lo, h, zero) for h in heads] + [jnp.where(lo, zero, h) for h in heads]
    qp = [h[c * pw:(c + 1) * pw] for h in masked_heads for c in range(ATTN_COL_SPLIT)]
    n_pieces = len(qp)
    m_scr[...] = jnp.full(m_scr.shape, NEG_INF, F32)
    l_scr[...] = jnp.zeros(l_scr.shape, F32)
    acc_scr[...] = jnp.zeros(acc_scr.shape, F32)

    def k_block(j):
        return k_ref[pl.ds(pl.multiple_of(j * tq, tq), tq), :]

    def vt_block(j):
        return vt_ref[:, pl.ds(pl.multiple_of(j * tq, tq), tq)]

    def produce_piece(dst, k, p, masked):
        s = lax.dot_general(k, qp[p], (((1,), (1,)), ((), ())), preferred_element_type=F32)
        if masked:
            kpos = lax.broadcasted_iota(I32, (tq, pw), 0)
            qpos = lax.broadcasted_iota(I32, (tq, pw), 1) + (p % ATTN_COL_SPLIT) * pw
            s = jnp.where(qpos >= kpos, s, NEG_INF)
        dst[p] = s

    def consume_piece(src, vt, p):
        s = src[p]
        m_prev = m_scr[p]
        m_new = jnp.maximum(m_prev, jnp.max(s, axis=0, keepdims=True))
        alpha = jnp.exp2(m_prev - m_new)
        e = jnp.exp2(s - m_new)
        l_scr[p] = alpha * l_scr[p] + jnp.sum(e, axis=0, keepdims=True)
        acc_scr[p] = alpha * acc_scr[p] + jnp.dot(vt, e.astype(BF16), preferred_element_type=F32)
        m_scr[p] = m_new

    def consume(src, kv_idx):
        vt = vt_block(kv_idx)
        for p in range(n_pieces):
            consume_piece(src, vt, p)

    def fused(src, kv_consume, dst, kv_produce):
        k = k_block(kv_produce)
        vt = vt_block(kv_consume)
        produce_piece(dst, k, 0, False)
        produce_piece(dst, k, 1, False)
        for p in range(n_pieces):
            consume_piece(src, vt, p)
            if p + 2 < n_pieces:
                produce_piece(dst, k, p + 2, False)

    k_diag = k_block(i)
    for p in range(n_pieces):
        produce_piece(s0_scr, k_diag, p, True)

    def body(u, carry):
        fused(s0_scr, jnp.where(u == 0, i, 2 * u - 1), s1_scr, 2 * u)
        fused(s1_scr, 2 * u, s0_scr, 2 * u + 1)
        return carry

    lax.fori_loop(0, i // 2, body, 0)
    odd = lax.rem(i, 2) == 1

    @pl.when(odd)
    def _():
        fused(s0_scr, jnp.where(i == 1, i, i - 2), s1_scr, i - 1)
        consume(s1_scr, i - 1)

    @pl.when(jnp.logical_not(odd))
    def _():
        consume(s0_scr, jnp.maximum(i - 1, 0))

    lam = lam_ref[0, 0]
    for g in range(GQA):
        for c in range(ATTN_COL_SPLIT):
            p1 = g * ATTN_COL_SPLIT + c
            p2 = (GQA + g) * ATTN_COL_SPLIT + c
            ot = acc_scr[p1] / l_scr[p1] - lam * (acc_scr[p2] / l_scr[p2])
            ot = ot * lax.rsqrt(jnp.mean(ot * ot, axis=0, keepdims=True) + EPS) * gsub_ref[...] * out_scale
            o_ref[c * pw:(c + 1) * pw, g * V_DIM:(g + 1) * V_DIM] = ot.T.astype(o_ref.dtype)


def _attn_prompt(lam, q, kb, vt, gsub_col, *, tq, out_scale):
    B, S, _ = q.shape
    n_pieces = 2 * GQA * ATTN_COL_SPLIT
    pw = tq // ATTN_COL_SPLIT
    return pl.pallas_call(
        functools.partial(_attn_prompt_kernel, tq=tq, out_scale=out_scale),
        grid=(B, N_KV_HEADS, S // tq),
        in_specs=[pl.BlockSpec(memory_space=pltpu.SMEM),
                  pl.BlockSpec((None, tq, GQA * LANES), lambda b, n, i: (b, i, n)),
                  pl.BlockSpec((None, S, LANES), lambda b, n, i: (b, 0, n)),
                  pl.BlockSpec((None, V_DIM, S), lambda b, n, i: (b, n, 0)),
                  pl.BlockSpec((V_DIM, 1), lambda b, n, i: (0, 0))],
        out_specs=pl.BlockSpec((None, tq, GQA * V_DIM), lambda b, n, i: (b, i, n)),
        out_shape=jax.ShapeDtypeStruct((B, S, N_HEADS * V_DIM), BF16),
        scratch_shapes=[pltpu.VMEM((n_pieces, 1, pw), F32), pltpu.VMEM((n_pieces, 1, pw), F32),
                        pltpu.VMEM((n_pieces, V_DIM, pw), F32),
                        pltpu.VMEM((n_pieces, tq, pw), F32), pltpu.VMEM((n_pieces, tq, pw), F32)],
        compiler_params=_params(("arbitrary", "arbitrary", "arbitrary")),
        name="attn_prompt",
    )(lam, q, kb, vt, gsub_col)


N_QROWS = 2 * N_HEADS
PAGE_ROWS = PAGE_SIZE * N_KV_HEADS
SAMPLE_LOOKAHEAD = 2


def _attn_sample_kernel(pt_ref, lam_ref, q_ref, kn_ref, vn_ref, gsub_ref, ck_ref, cv_ref, o_ref,
                        kbuf, vbuf, sems, m_scr, l_scr, acc_scr, *, pages, out_scale):
    n_slots = SAMPLE_LOOKAHEAD + 1
    b = pl.program_id(0)
    c = pl.program_id(1)
    n_c = pl.num_programs(1)
    step = b * n_c + c
    n_steps = pl.num_programs(0) * n_c
    slot = lax.rem(step, n_slots)

    def page_copies(s):
        bb = lax.div(s, n_c)
        cc = lax.rem(s, n_c)
        sl = lax.rem(s, n_slots)
        out = []
        for p in range(pages):
            page = pt_ref[bb, cc * pages + p]
            out.append(pltpu.make_async_copy(ck_ref.at[page], kbuf.at[sl, p], sems.at[sl]))
            out.append(pltpu.make_async_copy(cv_ref.at[page], vbuf.at[sl, p], sems.at[sl]))
        return out

    def start_all(copies):
        for n, cp in enumerate(copies):
            cp.start(priority=n % 2)

    for ahead in range(SAMPLE_LOOKAHEAD):
        @pl.when((step == 0) & (ahead < n_steps))
        def _():
            start_all(page_copies(jnp.int32(ahead)))

    @pl.when(step + SAMPLE_LOOKAHEAD < n_steps)
    def _():
        start_all(page_copies(step + SAMPLE_LOOKAHEAD))

    for cp in page_copies(step):
        cp.wait()

    @pl.when(c == 0)
    def _():
        m_scr[...] = jnp.full(m_scr.shape, NEG_INF, F32)
        l_scr[...] = jnp.zeros(l_scr.shape, F32)
        acc_scr[...] = jnp.zeros(acc_scr.shape, F32)

    q = q_ref[...]
    r = lax.broadcasted_iota(I32, (N_QROWS, PAGE_ROWS), 0)
    col = lax.broadcasted_iota(I32, (N_QROWS, PAGE_ROWS), 1)
    valid = (col & (N_KV_HEADS - 1)) == ((r & (N_HEADS - 1)) >> 1)
    k_pages = [kbuf[slot, p] for p in range(pages)]
    v_pages = [vbuf[slot, p] for p in range(pages)]
    scores = []
    for p in range(pages):
        s = lax.dot_general(q, k_pages[p].astype(BF16), (((1,), (1,)), ((), ())),
                            preferred_element_type=F32)
        scores.append(jnp.where(valid, s, NEG_INF))
    m_prev = m_scr[...]
    m_new = m_prev
    for s in scores:
        m_new = jnp.maximum(m_new, jnp.max(s, axis=-1, keepdims=True))
    alpha = jnp.exp2(m_prev - m_new)
    l_new = alpha * l_scr[...]
    acc = alpha * acc_scr[...]
    for p in range(pages):
        e = jnp.where(valid, jnp.exp2(scores[p] - m_new), 0.0)
        l_new = l_new + jnp.sum(e, axis=-1, keepdims=True)
        acc = acc + jnp.dot(e.astype(BF16), v_pages[p].astype(BF16), preferred_element_type=F32)
    m_scr[...] = m_new
    l_scr[...] = l_new
    acc_scr[...] = acc

    @pl.when(c == pl.num_programs(1) - 1)
    def _():
        s_self = jnp.sum(q.astype(F32) * kn_ref[...], axis=-1, keepdims=True)
        m_fin = jnp.maximum(m_new, s_self)
        a = jnp.exp2(m_new - m_fin)
        e_self = jnp.exp2(s_self - m_fin)
        o = (a * acc + e_self * vn_ref[...]) / (a * l_new + e_self)
        lam = lam_ref[0, 0]
        od = o[:N_HEADS] - lam * o[N_HEADS:]
        o_ref[...] = (_rmsnorm(od, gsub_ref[...]) * out_scale).astype(o_ref.dtype)


def _attn_sample(page_table, lam, q16, kn16, vn16, gsub, ck3, cv3, *, pages, out_scale):
    DB, n_pages = page_table.shape
    steps = n_pages // pages
    row3 = lambda n: pl.BlockSpec((None, n, LANES), lambda b, c, pt: (b, 0, 0))
    grid_spec = pltpu.PrefetchScalarGridSpec(
        num_scalar_prefetch=1,
        grid=(DB, steps),
        in_specs=[pl.BlockSpec(memory_space=pltpu.SMEM), row3(N_QROWS), row3(N_QROWS), row3(N_QROWS),
                  pl.BlockSpec((1, V_DIM), lambda b, c, pt: (0, 0)),
                  pl.BlockSpec(memory_space=pl.ANY), pl.BlockSpec(memory_space=pl.ANY)],
        out_specs=row3(N_HEADS),
        scratch_shapes=[pltpu.VMEM((SAMPLE_LOOKAHEAD + 1, pages, PAGE_ROWS, LANES), F32),
                        pltpu.VMEM((SAMPLE_LOOKAHEAD + 1, pages, PAGE_ROWS, LANES), F32),
                        pltpu.SemaphoreType.DMA((SAMPLE_LOOKAHEAD + 1,)),
                        pltpu.VMEM((N_QROWS, 1), F32), pltpu.VMEM((N_QROWS, 1), F32),
                        pltpu.VMEM((N_QROWS, V_DIM), F32)],
    )
    return pl.pallas_call(
        functools.partial(_attn_sample_kernel, pages=pages, out_scale=out_scale),
        grid_spec=grid_spec,
        out_shape=jax.ShapeDtypeStruct((DB, N_HEADS, V_DIM), BF16),
        compiler_params=_params(("arbitrary", "arbitrary")),
        name="attn_sample",
    )(page_table, lam, q16, kn16, vn16, gsub, ck3, cv3)


def _post_kernel(o_ref, sga_ref, gyb_ref, x_ref, gate_ref, shift_ref, scale_ref, g2_ref,
                 wba_ref, wout_ref, wr_ref, br_ref,
                 x1_ref, h2p_ref, route_ref, counts_ref, base_scr):
    tm = x_ref.shape[0]
    first = (pl.program_id(0) == 0) & (pl.program_id(1) == 0)

    @pl.when(first)
    def _():
        base_scr[...] = jnp.zeros(base_scr.shape, F32)

    ya = jnp.dot(o_ref[...], wba_ref[...], preferred_element_type=F32)
    merged = sga_ref[...].astype(F32) * ya + gyb_ref[...].astype(F32)
    upd = jnp.dot(merged.astype(BF16), wout_ref[...], preferred_element_type=F32)
    x1 = x_ref[...] + gate_ref[...] * upd
    x1_ref[...] = x1
    h2 = _rmsnorm(x1, g2_ref[...]) * (1.0 + scale_ref[...]) + shift_ref[...]
    half = D_MODEL // 2
    h2p_ref[...] = _pack_pair(h2[:, :half], h2[:, half:])

    h_hi = h2.astype(BF16)
    h_lo = (h2 - h_hi.astype(F32)).astype(BF16)
    logits = (jnp.dot(h_hi, wr_ref[0], preferred_element_type=F32)
              + (jnp.dot(h_hi, wr_ref[1], preferred_element_type=F32)
                 + jnp.dot(h_lo, wr_ref[0], preferred_element_type=F32))) + br_ref[...]
    lane = lax.broadcasted_iota(I32, (tm, LANES), 1)
    lanef = lane.astype(F32)
    big = float(LANES)
    is_g = lane < N_GROUPS
    lg = jnp.where(is_g, logits, -jnp.inf)
    gmax = jnp.max(lg, axis=-1, keepdims=True)
    g_idx = jnp.min(jnp.where(lg == gmax, lanef, big), axis=-1, keepdims=True)
    p_grp = 1.0 / jnp.sum(jnp.where(is_g, jnp.exp(logits - gmax), 0.0), axis=-1, keepdims=True)
    e_lo = N_GROUPS + EXPERTS_PER_GROUP * g_idx
    in_grp = (lanef >= e_lo) & (lanef < e_lo + EXPERTS_PER_GROUP)
    emax = jnp.max(jnp.where(in_grp, logits, -jnp.inf), axis=-1, keepdims=True)
    ee = jnp.where(in_grp, jnp.exp(logits - emax), 0.0)
    pe = ee / jnp.sum(ee, axis=-1, keepdims=True)
    pe_m = jnp.where(in_grp, pe, -1.0)
    v1 = jnp.max(pe_m, axis=-1, keepdims=True)
    i1 = jnp.min(jnp.where(pe_m == v1, lanef, big), axis=-1, keepdims=True)
    pe_m2 = jnp.where(lanef == i1, -1.0, pe_m)
    v2 = jnp.max(pe_m2, axis=-1, keepdims=True)
    i2 = jnp.min(jnp.where(pe_m2 == v2, lanef, big), axis=-1, keepdims=True)
    tsum = v1 + v2
    w1 = p_grp * (v1 / tsum)
    w2 = p_grp * (v2 / tsum)
    e1 = i1 - N_GROUPS
    e2 = i2 - N_GROUPS

    oh1 = (lanef == e1).astype(F32)
    oh2 = (lanef == e2).astype(F32)
    ohs = oh1 + oh2
    rr = lax.broadcasted_iota(I32, (tm, tm), 0)
    cc = lax.broadcasted_iota(I32, (tm, tm), 1)
    tri = jnp.where(rr > cc, 1.0, 0.0).astype(BF16)
    before = jnp.dot(tri, ohs.astype(BF16), preferred_element_type=F32) + base_scr[...]
    r1 = jnp.sum(before * oh1, axis=-1, keepdims=True)
    r2 = jnp.sum(before * oh2, axis=-1, keepdims=True)
    new_base = base_scr[...] + jnp.sum(ohs, axis=0, keepdims=True)
    base_scr[...] = new_base
    counts_ref[...] = jnp.broadcast_to(new_base, counts_ref.shape)

    route = jnp.zeros((tm, LANES), F32)
    for idx, val in enumerate((e1, e2, r1, r2, w1, w2)):
        route = jnp.where(lane == idx, val, route)
    route_ref[...] = route


def _post(o, sga, gyb, x3, gate, shift, scale, g2, wba, wout, wr, br, *, tm):
    B, S, D = x3.shape
    per_token = gate.shape[1] != 1
    mod_spec = (pl.BlockSpec((None, tm, D), lambda b, i: (b, i, 0)) if per_token
                else pl.BlockSpec((None, 1, D), lambda b, i: (b, 0, 0)))
    const2 = lambda a: pl.BlockSpec(a.shape, lambda b, i: (0,) * a.ndim, pipeline_mode=pl.Buffered(1))
    tok = lambda n: pl.BlockSpec((None, tm, n), lambda b, i: (b, i, 0))
    return pl.pallas_call(
        _post_kernel,
        grid=(B, S // tm),
        in_specs=[tok(D), tok(D), tok(D), tok(D), mod_spec, mod_spec, mod_spec, const2(g2),
                  const2(wba), const2(wout), const2(wr), const2(br)],
        out_specs=[tok(D), tok(D // 2), tok(LANES), pl.BlockSpec((8, LANES), lambda b, i: (0, 0))],
        out_shape=[jax.ShapeDtypeStruct((B, S, D), F32),
                   jax.ShapeDtypeStruct((B, S, D // 2), I32),
                   jax.ShapeDtypeStruct((B, S, LANES), F32),
                   jax.ShapeDtypeStruct((8, LANES), F32)],
        scratch_shapes=[pltpu.VMEM((1, LANES), F32)],
        compiler_params=_params(("arbitrary", "arbitrary")),
        name="post",
    )(o, sga, gyb, x3, gate, shift, scale, g2, wba, wout, wr, br)


ROW_UNROLL = 8


def _row_copy(src, dst, src_row, dst_row, sem):
    return pltpu.make_async_copy(src.at[pl.ds(src_row, 1)], dst.at[pl.ds(dst_row, 1)], sem)


def _dispatch_kernel(dest_ref, h_ref, xs_in_ref, xs_ref, dest_smem, sem, dsem):
    del xs_in_ref
    tm = h_ref.shape[0]
    cp = pltpu.make_async_copy(dest_ref.at[0], dest_smem, dsem)
    cp.start()
    cp.wait()

    def issue(blk, carry):
        for u in range(ROW_UNROLL):
            r = blk * ROW_UNROLL + u
            _row_copy(h_ref, xs_ref, r, dest_smem[0, 2 * r], sem).start(priority=0)
            _row_copy(h_ref, xs_ref, r, dest_smem[0, 2 * r + 1], sem).start(priority=1)
        return carry

    lax.fori_loop(0, tm // ROW_UNROLL, issue, 0)
    for _ in range(2):
        pltpu.make_async_copy(h_ref, xs_ref.at[pl.ds(0, tm)], sem).wait()


def _dispatch(dest, h2p, slots, *, tm):
    T, W = h2p.shape
    nt = T // tm
    xs0 = jnp.zeros((slots, W), I32)
    return pl.pallas_call(
        _dispatch_kernel,
        grid=(nt,),
        in_specs=[pl.BlockSpec((1, 1, 2 * tm), lambda i: (i, 0, 0)),
                  pl.BlockSpec((tm, W), lambda i: (i, 0)),
                  pl.BlockSpec(memory_space=pl.ANY)],
        out_specs=pl.BlockSpec(memory_space=pl.ANY),
        out_shape=jax.ShapeDtypeStruct((slots, W), I32),
        scratch_shapes=[pltpu.SMEM((1, 2 * tm), I32), pltpu.SemaphoreType.DMA(()), pltpu.SemaphoreType.DMA(())],
        input_output_aliases={2: 0},
        compiler_params=_params(("arbitrary",)),
        name="dispatch",
    )(dest.reshape(nt, 1, 2 * tm), h2p, xs0)


SC_GATHER_ROWS = 64


def _sc_gather(table, idx):
    info = plsc.get_sparse_core_info()
    n_workers = info.num_cores * info.num_subcores
    B = idx.shape[0]
    W = table.shape[1]
    R = SC_GATHER_ROWS
    per_worker = B // n_workers
    n_chunks = per_worker // R
    assert per_worker * n_workers == B and n_chunks * R == per_worker and n_chunks % 2 == 0
    mesh = plsc.VectorSubcoreMesh(core_axis_name="c", subcore_axis_name="s")

    def body(table_hbm, idx_hbm, out_hbm, idx_v, rows_v, sems):
        wid = lax.axis_index("s") * info.num_cores + lax.axis_index("c")
        base = wid * per_worker

        def gather(chunk, slot):
            off = pl.multiple_of(base + chunk * R, 8)
            pltpu.sync_copy(idx_hbm.at[pl.ds(off, R)], idx_v.at[slot])
            return pltpu.make_async_copy(table_hbm.at[idx_v.at[slot]], rows_v.at[slot], sems.at[slot])

        def flush(chunk, slot):
            off = pl.multiple_of(base + chunk * R, 8)
            pltpu.sync_copy(rows_v.at[slot], out_hbm.at[pl.ds(off, R)])

        gather(0, 0).start()

        @pl.loop(0, n_chunks, step=2)
        def _(g):
            gather(g + 1, 1).start()
            pltpu.make_async_copy(table_hbm.at[idx_v.at[0]], rows_v.at[0], sems.at[0]).wait()
            flush(g, 0)

            @pl.when(g + 2 < n_chunks)
            def _():
                gather(g + 2, 0).start()

            pltpu.make_async_copy(table_hbm.at[idx_v.at[1]], rows_v.at[1], sems.at[1]).wait()
            flush(g + 1, 1)

    return pl.kernel(
        body,
        out_type=jax.ShapeDtypeStruct((B, W), I32),
        mesh=mesh,
        scratch_types=[pltpu.VMEM((2, R), I32), pltpu.VMEM((2, R, W), I32), pltpu.SemaphoreType.DMA((2,))],
        name="sc_gather",
    )(table, idx)


def _ffn_kernel(be_ref, nu_ref, x_ref, wg_ref, wu_ref, wd_ref, y_ref):
    del be_ref
    i = pl.program_id(0)

    @pl.when(i < nu_ref[0])
    def _():
        a, b = _unpack_pair(x_ref[...])
        x = jnp.concatenate([a.astype(BF16), b.astype(BF16)], axis=1)
        hg = jnp.dot(x, wg_ref[...].astype(BF16), preferred_element_type=F32)
        hu = jnp.dot(x, wu_ref[...].astype(BF16), preferred_element_type=F32)
        hid = (hg * jax.nn.sigmoid(hg)) * hu
        y = jnp.dot(hid.astype(BF16), wd_ref[...].astype(BF16), preferred_element_type=F32)
        half = D_MODEL // 2
        y_ref[...] = _pack_pair(y[:, :half], y[:, half:])

    @pl.when(i >= nu_ref[0])
    def _():
        y_ref[...] = jnp.zeros(y_ref.shape, I32)


def _ffn(block_expert, n_used, xs, weg, weu, wed, *, bm):
    slots, W = xs.shape
    nb = slots // bm
    grid_spec = pltpu.PrefetchScalarGridSpec(
        num_scalar_prefetch=2,
        grid=(nb,),
        in_specs=[pl.BlockSpec((bm, W), lambda i, be, nu: (i, 0)),
                  pl.BlockSpec((None, D_MODEL, D_EXPERT), lambda i, be, nu: (be[i], 0, 0)),
                  pl.BlockSpec((None, D_MODEL, D_EXPERT), lambda i, be, nu: (be[i], 0, 0)),
                  pl.BlockSpec((None, D_EXPERT, D_MODEL), lambda i, be, nu: (be[i], 0, 0))],
        out_specs=pl.BlockSpec((bm, W), lambda i, be, nu: (i, 0)),
    )
    return pl.pallas_call(
        _ffn_kernel,
        grid_spec=grid_spec,
        out_shape=jax.ShapeDtypeStruct((slots, W), I32),
        compiler_params=_params(("arbitrary",)),
        name="ffn",
    )(block_expert, n_used, xs, weg, weu, wed)


def _combine_kernel(dest_ref, dest_next_ref, ys_ref, x1_ref, route_ref, gate_ref, fshift_ref, fscale_ref,
                    gf_ref, o_ref, dest_smem, rows_scr, sems, dsem):
    tm = x1_ref.shape[0]
    step = pl.program_id(0) * pl.num_programs(1) + pl.program_id(1)
    n_steps = pl.num_programs(0) * pl.num_programs(1)
    slot = lax.rem(step, 2)

    def gather(dref, sl):
        cp = pltpu.make_async_copy(dref.at[0], dest_smem, dsem)
        cp.start()
        cp.wait()

        def issue(blk, carry):
            for u in range(ROW_UNROLL):
                r = blk * ROW_UNROLL + u
                _row_copy(ys_ref, rows_scr.at[sl, 0], dest_smem[0, 2 * r], r, sems.at[sl]).start(priority=0)
                _row_copy(ys_ref, rows_scr.at[sl, 1], dest_smem[0, 2 * r + 1], r, sems.at[sl]).start(priority=1)
            return carry

        lax.fori_loop(0, tm // ROW_UNROLL, issue, 0)

    @pl.when(step == 0)
    def _():
        gather(dest_ref, 0)

    @pl.when(step + 1 < n_steps)
    def _():
        gather(dest_next_ref, 1 - slot)

    for half in range(2):
        pltpu.make_async_copy(ys_ref.at[pl.ds(0, tm)], rows_scr.at[slot, half], sems.at[slot]).wait()

    o_ref[...] = _combine_math(rows_scr[slot, 0], rows_scr[slot, 1], x1_ref[...], route_ref[...],
                               gate_ref[...], fshift_ref[...], fscale_ref[...], gf_ref[...])


def _combine_math(rows1, rows2, x1, route, gate, fshift, fscale, gf):
    w1 = route[:, 4:5]
    w2 = route[:, 5:6]
    a1, b1 = _unpack_pair(rows1)
    a2, b2 = _unpack_pair(rows2)
    y = jnp.concatenate([a1 * w1 + a2 * w2, b1 * w1 + b2 * w2], axis=1)
    x2 = x1 + gate * y
    return _rmsnorm(x2, gf) * (1.0 + fscale) + fshift


def _combine_dense_kernel(rows_ref, x1_ref, route_ref, gate_ref, fshift_ref, fscale_ref, gf_ref, o_ref):
    o_ref[...] = _combine_math(rows_ref[0], rows_ref[1], x1_ref[...], route_ref[...],
                               gate_ref[...], fshift_ref[...], fscale_ref[...], gf_ref[...])


def _combine_dense(rows, x1, route, gate, fshift, fscale, gf, *, tm):
    B, S, D = x1.shape
    W = rows.shape[-1]
    per_token = gate.shape[1] != 1
    mod_spec = (pl.BlockSpec((None, tm, D), lambda b, i: (b, i, 0)) if per_token
                else pl.BlockSpec((None, 1, D), lambda b, i: (b, 0, 0)))
    tok = lambda n: pl.BlockSpec((None, tm, n), lambda b, i: (b, i, 0))
    return pl.pallas_call(
        _combine_dense_kernel,
        grid=(B, S // tm),
        in_specs=[pl.BlockSpec((2, None, tm, W), lambda b, i: (0, b, i, 0)),
                  tok(D), tok(LANES), mod_spec, mod_spec, mod_spec,
                  pl.BlockSpec(gf.shape, lambda b, i: (0, 0))],
        out_specs=tok(D),
        out_shape=jax.ShapeDtypeStruct((B, S, D), F32),
        compiler_params=_params(("arbitrary", "arbitrary")),
        name="combine_dense",
    )(rows, x1, route, gate, fshift, fscale, gf)


def _combine(dest, ys, x1, route, gate, fshift, fscale, gf, *, tm):
    B, S, D = x1.shape
    nt = S // tm
    W = ys.shape[1]
    per_token = gate.shape[1] != 1
    mod_spec = (pl.BlockSpec((None, tm, D), lambda b, i: (b, i, 0)) if per_token
                else pl.BlockSpec((None, 1, D), lambda b, i: (b, 0, 0)))
    tok = lambda n: pl.BlockSpec((None, tm, n), lambda b, i: (b, i, 0))
    dest3 = dest.reshape(B * nt, 1, 2 * tm)
    return pl.pallas_call(
        _combine_kernel,
        grid=(B, nt),
        in_specs=[pl.BlockSpec((1, 1, 2 * tm), lambda b, i: (b * nt + i, 0, 0)),
                  pl.BlockSpec((1, 1, 2 * tm), lambda b, i: (jnp.minimum(b * nt + i + 1, B * nt - 1), 0, 0)),
                  pl.BlockSpec(memory_space=pl.ANY),
                  tok(D), tok(LANES), mod_spec, mod_spec, mod_spec,
                  pl.BlockSpec(gf.shape, lambda b, i: (0, 0))],
        out_specs=tok(D),
        out_shape=jax.ShapeDtypeStruct((B, S, D), F32),
        scratch_shapes=[pltpu.SMEM((1, 2 * tm), I32), pltpu.VMEM((2, 2, tm, W), I32),
                        pltpu.SemaphoreType.DMA((2,)), pltpu.SemaphoreType.DMA(())],
        compiler_params=_params(("arbitrary", "arbitrary")),
        name="combine",
    )(dest3, dest3, ys, x1, route, gate, fshift, fscale, gf)


def _rope_tables(pos):
    half = ROPE_DIM // 2
    inv_freq = ROPE_THETA ** (-jnp.arange(0, ROPE_DIM, 2, dtype=F32) / ROPE_DIM)
    ang = pos.astype(F32)[:, None] * inv_freq[None, :]
    cos, sin = jnp.cos(ang), jnp.sin(ang)
    n = pos.shape[0]
    rest = HEAD_DIM - ROPE_DIM
    c64 = jnp.concatenate([cos, cos, jnp.ones((n, rest), F32)], axis=1)
    a64 = jnp.concatenate([-sin, jnp.zeros((n, HEAD_DIM - half), F32)], axis=1)
    b64 = jnp.concatenate([jnp.zeros((n, half), F32), sin, jnp.zeros((n, rest), F32)], axis=1)
    return tuple(jnp.tile(t, (1, LANES // HEAD_DIM)) for t in (c64, a64, b64))


def _moe_plan(route, counts, bm):
    T = route.shape[0]
    e = route[:, 0:2].astype(I32)
    rank = route[:, 2:4].astype(I32)
    cnt = counts[0, :N_EXPERTS].astype(I32)
    padded = (cnt + bm - 1) // bm * bm
    pad_end = jnp.cumsum(padded)
    pad_start = pad_end - padded
    ids = jnp.arange(N_EXPERTS, dtype=I32)
    dest = rank + jnp.sum(jnp.where(e[..., None] == ids, pad_start, 0), axis=-1)
    n_blocks = -(-(2 * T) // bm) + N_EXPERTS
    block_lo = jnp.arange(n_blocks, dtype=I32) * bm
    block_expert = jnp.minimum(jnp.sum((pad_end[None, :] <= block_lo[:, None]).astype(I32), axis=1),
                               N_EXPERTS - 1)
    n_used = (pad_end[-1:] // bm).astype(I32)
    return dest.reshape(-1), block_expert, n_used, n_blocks * bm


def _slot_tokens(route, counts, bm, slots):
    T = route.shape[0]
    e_flat = route[:, 0:2].astype(I32).reshape(-1)
    cnt = counts[0, :N_EXPERTS].astype(I32)
    padded = (cnt + bm - 1) // bm * bm
    pad_end = jnp.cumsum(padded)
    pad_start = pad_end - padded
    start = jnp.cumsum(cnt) - cnt
    order = jnp.argsort(e_flat, stable=True).astype(I32)
    s = jnp.arange(slots, dtype=I32)
    eb = jnp.minimum(jnp.sum((pad_end[None, :] <= s[:, None]).astype(I32), axis=1), N_EXPERTS - 1)
    hot = eb[:, None] == jnp.arange(N_EXPERTS, dtype=I32)
    pick = lambda tab: jnp.sum(jnp.where(hot, tab, 0), axis=1)
    r = s - pick(pad_start)
    valid = r < pick(cnt)
    src = jnp.clip(pick(start) + r, 0, 2 * T - 1)
    return jnp.where(valid, order[src] // 2, s % T)


def _layer(x3, mod, modf, pos_tabs, attend, lam_init, wts, *, tm, tmr, bm, chunked, emit_gv, sc_rows):
    B, S, D = x3.shape
    m = lambda k: mod[:, :, k, :]
    outs = _inproj(x3, m(0), m(1), wts['g1'], wts['win'], pos_tabs, wts['wsp'], wts['bsp'],
                   wts['gln'], wts['bln'], wts['wbb'], tm=tm, chunked=chunked, emit_gv=emit_gv)
    q, kf, vf, kb, vt, sga, gyb = outs[:7]
    gv = outs[7] if emit_gv else None
    o = attend(q, kf, vf, kb, vt)
    x1, h2p, route, counts = _post(o, sga, gyb, x3, m(2), m(3), m(4), wts['g2'], wts['wba'],
                                   wts['wout'], wts['wr'], wts['br'], tm=tm)
    T = B * S
    W = D // 2
    dest, block_expert, n_used, slots = _moe_plan(route.reshape(T, LANES), counts, bm)
    fin = (m(5), modf[:, :, 0, :], modf[:, :, 1, :], wts['gf'])
    if sc_rows:
        xs = _sc_gather(h2p.reshape(T, W), _slot_tokens(route.reshape(T, LANES), counts, bm, slots))
        ys = _ffn(block_expert, n_used, xs, wts['weg'], wts['weu'], wts['wed'], bm=bm)
        rows = _sc_gather(ys, dest.reshape(T, 2).T.reshape(-1)).reshape(2, B, S, W)
        y = _combine_dense(rows, x1, route, *fin, tm=tm)
    else:
        xs = _dispatch(dest, h2p.reshape(T, W), slots, tm=tmr)
        ys = _ffn(block_expert, n_used, xs, wts['weg'], wts['weu'], wts['wed'], bm=bm)
        y = _combine(dest, ys, x1, route, *fin, tm=tmr)
    return y, kf, vf, gv


def kernel(x_prompt, x_sample, cache_k, cache_v, page_table, c_prompt, c_sample, w_ada, b_ada, w_ada_final, b_ada_final, g_norm1, g_norm2, g_final, w_in, lambda_q1, lambda_k1, lambda_q2, lambda_k2, g_subln, g_gmlp_ln, b_gmlp_ln, w_spatial, b_spatial, w_branch_a, w_branch_b, w_out, w_router_group, b_router_group, w_router_expert, b_router_expert, w_exp_gate, w_exp_up, w_exp_down):
    depth = w_ada.shape[0]
    assert depth == 1
    B, S, D = x_prompt.shape
    DB, DS, _ = x_sample.shape
    assert DS == 1 and D == D_MODEL
    n_pages = page_table.shape[1]
    n_past = n_pages * PAGE_SIZE
    l = 0
    lam_init = 0.8 - 0.6 * math.exp(-0.3 * l)
    out_scale = 1.0 - lam_init

    c_all = jnp.concatenate([c_prompt, c_sample], axis=0)
    mod_all = _adaln(c_all, w_ada[l], b_ada[l])
    modf_all = _adaln(c_all, w_ada_final, b_ada_final)
    mod_p = mod_all[:B].reshape(B, 1, 6, D)
    mod_s = mod_all[B:].reshape(1, DB, 6, D)
    modf_p = modf_all[:B].reshape(B, 1, 2, D)
    modf_s = modf_all[B:].reshape(1, DB, 2, D)
    lam = _lam(lambda_q1[l], lambda_k1[l], lambda_q2[l], lambda_k2[l], lam_init)[0:1, 0:1]

    row = lambda a: a.reshape(1, -1)
    wr = jnp.zeros((D, LANES), F32)
    wr = wr.at[:, :N_GROUPS].set(w_router_group[l]).at[:, N_GROUPS:N_GROUPS + N_EXPERTS].set(w_router_expert[l])
    wr_hi = wr.astype(BF16)
    wr_lo = (wr - wr_hi.astype(F32)).astype(BF16)
    br = jnp.zeros((1, LANES), F32)
    br = br.at[0, :N_GROUPS].set(b_router_group[l]).at[0, N_GROUPS:N_GROUPS + N_EXPERTS].set(b_router_expert[l])
    wts = dict(
        g1=row(g_norm1[l]), g2=row(g_norm2[l]), gf=row(g_final), win=w_in[l].astype(BF16),
        gln=row(g_gmlp_ln[l]), bln=row(b_gmlp_ln[l]), wbb=w_branch_b[l].astype(BF16),
        wba=w_branch_a[l].astype(BF16), wout=w_out[l].astype(BF16), wr=jnp.stack([wr_hi, wr_lo]), br=br,
        weg=w_exp_gate[l], weu=w_exp_up[l], wed=w_exp_down[l])
    gsub = row(g_subln[l])

    wts_s = dict(wts, wsp=row(jnp.repeat(w_spatial[l][:, 0, 0], GMLP_GROUP_DIM)),
                 bsp=row(jnp.repeat(b_spatial[l][:, 0], GMLP_GROUP_DIM)))
    tabs_s = _rope_tables(jnp.full((1,), n_past, dtype=I32))
    ck3 = cache_k[l].reshape(-1, PAGE_ROWS, LANES)
    cv3 = cache_v[l].reshape(-1, PAGE_ROWS, LANES)
    half_mask = (jnp.arange(LANES)[None, :] // HEAD_DIM == jnp.arange(2)[:, None])

    def attend_s(q, kf, vf, kb, vb):
        del kb, vb
        qh = q.reshape(DB, 1, N_HEADS, LANES)
        q16 = jnp.where(half_mask[None, :, None, :], qh, jnp.zeros_like(qh)).reshape(DB, N_QROWS, LANES)
        kv_of_row = (jnp.arange(N_QROWS) % N_HEADS) // GQA
        kn16 = kf.reshape(DB, N_KV_HEADS, LANES)[:, kv_of_row]
        vn16 = vf.reshape(DB, N_KV_HEADS, V_DIM)[:, kv_of_row]
        pages = next(p for p in (16, 8, 4, 2, 1) if n_pages % p == 0)
        o = _attn_sample(page_table, lam, q16, kn16, vn16, gsub, ck3, cv3, pages=pages, out_scale=out_scale)
        return o.reshape(1, DB, N_HEADS * V_DIM)

    y_s, k_s, v_s, gv_s = _layer(x_sample.reshape(1, DB, D), mod_s, modf_s, tabs_s, attend_s, lam_init,
                                 wts_s, tm=DB, tmr=DB, bm=128, chunked=False, emit_gv=True, sc_rows=False)

    wts_p = dict(wts, wsp=w_spatial[l],
                 bsp=jnp.broadcast_to(b_spatial[l][:, :, None], (GMLP_GROUPS, CHUNK, GMLP_GROUP_DIM)))
    tabs_p = _rope_tables(jnp.arange(S, dtype=I32))
    tq = min(512, S)
    gsub_col = g_subln[l].reshape(V_DIM, 1)
    attend_p = lambda q, kf, vf, kb, vt: _attn_prompt(lam, q, kb, vt, gsub_col, tq=tq, out_scale=out_scale)
    y_p, k_p, v_p, _ = _layer(x_prompt, mod_p, modf_p, tabs_p, attend_p, lam_init, wts_p,
                              tm=min(512, S), tmr=min(256, S), bm=512, chunked=True, emit_gv=False,
                              sc_rows=True)

    return (y_p,
            y_s.reshape(DB, 1, D),
            k_p.reshape(1, B, S, N_KV_HEADS, 2 * HEAD_DIM),
            v_p.reshape(1, B, S, N_KV_HEADS, V_DIM),
            k_s.reshape(1, DB, 1, N_KV_HEADS, 2 * HEAD_DIM),
            v_s.reshape(1, DB, 1, N_KV_HEADS, V_DIM),
            gv_s.reshape(1, DB, 1, D))
```
